```python
import math
import jax, jax.numpy as jnp
from jax import lax
import numpy as np

D_MODEL = 4096
BATCH = 1
SEQ = 16384
DEPTH = 4

N_MIXERS = 3
N_SWA_LAYERS = (DEPTH + 2) // 3
N_DIFF_LAYERS = (DEPTH + 1) // 3
N_POOL_LAYERS = DEPTH // 3

HEAD_DIM = 128
ROPE_DIM = HEAD_DIM // 4
ROPE_THETA = 500000.0
BLOCK = 128

SWA_Q_HEADS = D_MODEL // HEAD_DIM
SWA_KV_HEADS = 8
SWA_GROUP = SWA_Q_HEADS // SWA_KV_HEADS
WINDOW = 128

DIFF_HEADS = D_MODEL // (2 * HEAD_DIM)
DIFF_KV_HEADS = 4
DIFF_GROUP = DIFF_HEADS // DIFF_KV_HEADS
DIFF_V_DIM = 2 * HEAD_DIM

POOL_GROUPS = 4
POOL_WINDOWS = (2, 4, 8, 16)
POOL_CH = D_MODEL // POOL_GROUPS

MEM_LEN = 256
X_HEADS = 4
X_HEAD_DIM = 256
X_WIDTH = X_HEADS * X_HEAD_DIM

D_FF = 3 * D_MODEL
CONV_WIDTH = 3

NORM_EPS = 1e-5

kernel_name = "hybrid_swa_diff_pool_convffn_trunk"


def rms_norm(x, g):
    xf = x.astype(jnp.float32)
    y = xf * lax.rsqrt(jnp.mean(xf * xf, axis=-1, keepdims=True) + NORM_EPS)
    return (y * g.astype(jnp.float32)).astype(x.dtype)


def rope_tables(positions):
    inv_freq = ROPE_THETA ** (-jnp.arange(0, ROPE_DIM, 2, dtype=jnp.float32) / ROPE_DIM)
    ang = positions.astype(jnp.float32)[..., None] * inv_freq
    return jnp.cos(ang), jnp.sin(ang)


def apply_partial_rope(t, cos, sin):
    shape = cos.shape[:2] + (1,) * (t.ndim - 3) + cos.shape[-1:]
    c, s = cos.reshape(shape), sin.reshape(shape)
    tf = t.astype(jnp.float32)
    half = ROPE_DIM // 2
    t1, t2 = tf[..., :half], tf[..., half:ROPE_DIM]
    out = jnp.concatenate([t1 * c - t2 * s, t2 * c + t1 * s, tf[..., ROPE_DIM:]], axis=-1)
    return out.astype(t.dtype)


def swa_sink_attention(h, cos, sin, w_qkv, sinks, w_o):
    B, S, _ = h.shape
    nb = S // BLOCK
    qkv = h @ w_qkv
    q, k, v = jnp.split(qkv, [SWA_Q_HEADS * HEAD_DIM, (SWA_Q_HEADS + SWA_KV_HEADS) * HEAD_DIM], axis=-1)
    q = apply_partial_rope(q.reshape(B, S, SWA_KV_HEADS, SWA_GROUP, HEAD_DIM), cos, sin)
    k = apply_partial_rope(k.reshape(B, S, SWA_KV_HEADS, HEAD_DIM), cos, sin)
    v = v.reshape(B, S, SWA_KV_HEADS, HEAD_DIM)
    qb = q.reshape(B, nb, BLOCK, SWA_KV_HEADS, SWA_GROUP, HEAD_DIM)

    def with_prev(t):
        tb = t.reshape(B, nb, BLOCK, SWA_KV_HEADS, HEAD_DIM)
        prev = jnp.pad(tb, ((0, 0), (1, 0), (0, 0), (0, 0), (0, 0)))[:, :-1]
        return jnp.concatenate([prev, tb], axis=2)

    kw, vw = with_prev(k), with_prev(v)
    scores = jnp.einsum('bnqhgd,bnkhd->bnhgqk', qb, kw,
                        preferred_element_type=jnp.float32) * (HEAD_DIM ** -0.5)
    qi = jnp.arange(BLOCK)[:, None]
    rel = jnp.arange(2 * BLOCK)[None, :] - BLOCK
    band = (rel <= qi) & (qi - rel < WINDOW)
    first = jnp.arange(nb)[:, None, None] == 0
    mask = band[None] & ~(first & (rel[None] < 0))
    scores = jnp.where(mask[None, :, None, None], scores, -jnp.inf)
    sink = jnp.broadcast_to(sinks.astype(jnp.float32).reshape(1, 1, SWA_KV_HEADS, SWA_GROUP, 1, 1),
                            scores.shape[:-1] + (1,))
    probs = jax.nn.softmax(jnp.concatenate([scores, sink], axis=-1), axis=-1)[..., :-1]
    out = jnp.einsum('bnhgqk,bnkhd->bnqhgd', probs.astype(vw.dtype), vw)
    return out.reshape(B, S, D_MODEL) @ w_o


def diff_attention(h, cos, sin, w_qkv, lq1, lk1, lq2, lk2, subln, w_o, lambda_init):
    B, S, _ = h.shape
    nb = S // BLOCK
    qkv = h @ w_qkv
    q, k, v = jnp.split(qkv, [DIFF_HEADS * 2 * HEAD_DIM,
                              DIFF_HEADS * 2 * HEAD_DIM + DIFF_KV_HEADS * 2 * HEAD_DIM], axis=-1)
    q = apply_partial_rope(q.reshape(B, S, DIFF_KV_HEADS, DIFF_GROUP, 2, HEAD_DIM), cos, sin)
    k = apply_partial_rope(k.reshape(B, S, DIFF_KV_HEADS, 2, HEAD_DIM), cos, sin)
    v = v.reshape(B, S, DIFF_KV_HEADS, DIFF_V_DIM)
    f32 = jnp.float32
    lam = (jnp.exp(jnp.sum(lq1.astype(f32) * lk1.astype(f32)))
           - jnp.exp(jnp.sum(lq2.astype(f32) * lk2.astype(f32))) + lambda_init)
    qb = jnp.moveaxis(q.reshape(B, nb, BLOCK, DIFF_KV_HEADS, DIFF_GROUP, 2, HEAD_DIM), 1, 0)
    kpos = jnp.arange(S)

    def block_fn(args):
        q_blk, n = args
        s = jnp.einsum('bqhgmd,bkhmd->bhgmqk', q_blk, k,
                       preferred_element_type=jnp.float32) * (HEAD_DIM ** -0.5)
        qpos = n * BLOCK + jnp.arange(BLOCK)
        s = jnp.where(kpos[None, :] <= qpos[:, None], s, -jnp.inf)
        p = jax.nn.softmax(s, axis=-1)
        p = p[:, :, :, 0] - lam * p[:, :, :, 1]
        return jnp.einsum('bhgqk,bkhe->bqhge', p.astype(v.dtype), v)

    out = lax.map(block_fn, (qb, jnp.arange(nb)))
    out = jnp.moveaxis(out, 0, 1).reshape(B, S, DIFF_HEADS, DIFF_V_DIM)
    out = rms_norm(out, subln) * (1.0 - lambda_init)
    return out.reshape(B, S, D_MODEL) @ w_o


def pool_mixer(h, w_pool, scale):
    B, S, _ = h.shape
    hf = h.astype(jnp.float32).reshape(B, S, POOL_GROUPS, POOL_CH)
    count = jnp.arange(1, S + 1, dtype=jnp.float32)[None, :, None]
    pooled = []
    for g, w in enumerate(POOL_WINDOWS):
        hg = hf[:, :, g]
        cs = jnp.cumsum(hg, axis=1)
        lagged = jnp.pad(cs, ((0, 0), (w, 0), (0, 0)))[:, :S]
        mean = (cs - lagged) / jnp.minimum(count, float(w))
        pooled.append(mean - hg)
    p = jnp.stack(pooled, axis=2).astype(h.dtype)
    y = jnp.einsum('bsgc,gce->bsge', p, w_pool).reshape(B, S, D_MODEL)
    return y * scale


def memory_cross_attention(h, mem_n, w_q, w_kv, w_o):
    B, S, _ = h.shape
    q = (h @ w_q).reshape(B, S, X_HEADS, X_HEAD_DIM)
    k, v = jnp.split(mem_n @ w_kv, 2, axis=-1)
    k = k.reshape(B, MEM_LEN, X_HEADS, X_HEAD_DIM)
    v = v.reshape(B, MEM_LEN, X_HEADS, X_HEAD_DIM)
    s = jnp.einsum('bshd,bmhd->bhsm', q, k, preferred_element_type=jnp.float32) * (X_HEAD_DIM ** -0.5)
    p = jax.nn.softmax(s, axis=-1)
    o = jnp.einsum('bhsm,bmhd->bshd', p.astype(v.dtype), v).reshape(B, S, X_WIDTH)
    return o @ w_o


def conv_ffn(h, w_in, conv_w, conv_b, w_out):
    S = h.shape[1]
    u = h @ w_in
    up = jnp.pad(u, ((0, 0), (CONV_WIDTH - 1, 0), (0, 0)))
    c = conv_b + conv_w[0] * up[:, 0:S]
    for j in range(1, CONV_WIDTH):
        c = c + conv_w[j] * up[:, j:j + S]
    gate, val = jnp.split(c, 2, axis=-1)
    return (jax.nn.silu(gate) * val) @ w_out


def setup_inputs(seed: int = 0) -> dict:
    key = jax.random.key(seed)
    ks = iter(jax.random.split(key, 40))
    f32 = jnp.float32

    def dense(shape, fan_in):
        return jax.random.normal(next(ks), shape, f32) * (fan_in ** -0.5)

    def gain(shape):
        return 1.0 + 0.02 * jax.random.normal(next(ks), shape, f32)

    x = jax.random.normal(next(ks), (BATCH, SEQ, D_MODEL), f32)
    mem = jax.random.normal(next(ks), (BATCH, MEM_LEN, D_MODEL), f32)
    offset = jax.random.randint(next(ks), (BATCH, 1), 0, 4096, dtype=jnp.int32)
    positions = offset + jnp.arange(SEQ, dtype=jnp.int32)[None, :]
    qkv_w = (SWA_Q_HEADS + 2 * SWA_KV_HEADS) * HEAD_DIM
    dqkv_w = 2 * DIFF_HEADS * HEAD_DIM + 2 * DIFF_KV_HEADS * HEAD_DIM + DIFF_KV_HEADS * DIFF_V_DIM
    return {
        "x": x,
        "mem": mem,
        "positions": positions,
        "swa_norm": gain((N_SWA_LAYERS, D_MODEL)),
        "swa_w_qkv": dense((N_SWA_LAYERS, D_MODEL, qkv_w), D_MODEL),
        "swa_sinks": jax.random.normal(next(ks), (N_SWA_LAYERS, SWA_Q_HEADS), f32),
        "swa_w_o": dense((N_SWA_LAYERS, D_MODEL, D_MODEL), D_MODEL),
        "diff_norm": gain((N_DIFF_LAYERS, D_MODEL)),
        "diff_w_qkv": dense((N_DIFF_LAYERS, D_MODEL, dqkv_w), D_MODEL),
        "diff_lambda_q1": 0.1 * jax.random.normal(next(ks), (N_DIFF_LAYERS, HEAD_DIM), f32),
        "diff_lambda_k1": 0.1 * jax.random.normal(next(ks), (N_DIFF_LAYERS, HEAD_DIM), f32),
        "diff_lambda_q2": 0.1 * jax.random.normal(next(ks), (N_DIFF_LAYERS, HEAD_DIM), f32),
        "diff_lambda_k2": 0.1 * jax.random.normal(next(ks), (N_DIFF_LAYERS, HEAD_DIM), f32),
        "diff_subln": gain((N_DIFF_LAYERS, DIFF_V_DIM)),
        "diff_w_o": dense((N_DIFF_LAYERS, D_MODEL, D_MODEL), D_MODEL),
        "pool_norm": gain((N_POOL_LAYERS, D_MODEL)),
        "pool_w": dense((N_POOL_LAYERS, POOL_GROUPS, POOL_CH, POOL_CH), POOL_CH),
        "pool_scale": gain((N_POOL_LAYERS, D_MODEL)),
        "xattn_norm": gain((DEPTH, D_MODEL)),
        "xattn_mem_norm": gain((DEPTH, D_MODEL)),
        "xattn_w_q": dense((DEPTH, D_MODEL, X_WIDTH), D_MODEL),
        "xattn_w_kv": dense((DEPTH, D_MODEL, 2 * X_WIDTH), D_MODEL),
        "xattn_w_o": dense((DEPTH, X_WIDTH, D_MODEL), X_WIDTH),
        "ffn_norm": gain((DEPTH, D_MODEL)),
        "ffn_w_in": dense((DEPTH, D_MODEL, 2 * D_FF), D_MODEL),
        "ffn_conv_w": dense((DEPTH, CONV_WIDTH, 2 * D_FF), CONV_WIDTH),
        "ffn_conv_b": 0.02 * jax.random.normal(next(ks), (DEPTH, 2 * D_FF), f32),
        "ffn_w_out": dense((DEPTH, D_FF, D_MODEL), D_FF),
        "final_norm": gain((D_MODEL,)),
    }


def reference(x, mem, positions, swa_norm, swa_w_qkv, swa_sinks, swa_w_o,
              diff_norm, diff_w_qkv, diff_lambda_q1, diff_lambda_k1, diff_lambda_q2,
              diff_lambda_k2, diff_subln, diff_w_o, pool_norm, pool_w, pool_scale,
              xattn_norm, xattn_mem_norm, xattn_w_q, xattn_w_kv, xattn_w_o,
              ffn_norm, ffn_w_in, ffn_conv_w, ffn_conv_b, ffn_w_out, final_norm):
    cos, sin = rope_tables(positions)
    for i in range(DEPTH):
        kind = i % N_MIXERS
        j = i // N_MIXERS
        if kind == 0:
            h = rms_norm(x, swa_norm[j])
            x = x + swa_sink_attention(h, cos, sin, swa_w_qkv[j], swa_sinks[j], swa_w_o[j])
        elif kind == 1:
            lambda_init = 0.8 - 0.6 * math.exp(-0.3 * i)
            h = rms_norm(x, diff_norm[j])
            x = x + diff_attention(h, cos, sin, diff_w_qkv[j], diff_lambda_q1[j], diff_lambda_k1[j],
                                   diff_lambda_q2[j], diff_lambda_k2[j], diff_subln[j], diff_w_o[j],
                                   lambda_init)
        else:
            h = rms_norm(x, pool_norm[j])
            x = x + pool_mixer(h, pool_w[j], pool_scale[j])
        h = rms_norm(x, xattn_norm[i])
        m = rms_norm(mem, xattn_mem_norm[i])
        x = x + memory_cross_attention(h, m, xattn_w_q[i], xattn_w_kv[i], xattn_w_o[i])
        h = rms_norm(x, ffn_norm[i])
        x = x + conv_ffn(h, ffn_w_in[i], ffn_conv_w[i], ffn_conv_b[i], ffn_w_out[i])
    return rms_norm(x, final_norm)
```

```python
import functools
import math

import jax
import jax.numpy as jnp
from jax import lax
from jax.experimental import pallas as pl
from jax.experimental.pallas import tpu as pltpu

HEAD_DIM = 128
ROPE_DIM = HEAD_DIM // 4
ROPE_HALF = ROPE_DIM // 2
ROPE_THETA = 500000.0
SWA_BLOCK = 128
POOL_WINDOWS = (2, 4, 8, 16)
POOL_HALO = 16
X_HEADS = 4
CONV_WIDTH = 3
CONV_HALO = 8
N_MIXERS = 3
NORM_EPS = 1e-5
LANES = 128

V7X_VMEM_BYTES = 64 * 2**20
VMEM_RESERVE_BYTES = 6 * 2**20

F32 = jnp.float32
BF16 = jnp.bfloat16


def _params(semantics, vmem_bytes):
    limit = min(int(vmem_bytes) * 5 // 4 + VMEM_RESERVE_BYTES, V7X_VMEM_BYTES - VMEM_RESERVE_BYTES // 2)
    return pltpu.CompilerParams(dimension_semantics=semantics, vmem_limit_bytes=limit)


def _tile(dim, pref, align):
    if dim <= pref:
        return dim
    t = (pref // align) * align
    while t > align and dim % t:
        t -= align
    assert dim % t == 0, (dim, pref, align)
    return t


def _rms(x, g, eps=NORM_EPS):
    ms = jnp.mean(x * x, axis=-1, keepdims=True)
    return x * lax.rsqrt(ms + eps) * g


def _rope_table_kernel(pos_ref, invf_ref, cos_ref, sa_ref, sb_ref):
    ang = pos_ref[...].astype(F32) * invf_ref[...]
    lane = lax.broadcasted_iota(jnp.int32, ang.shape, 1)
    s = jnp.sin(ang)
    cos_ref[...] = jnp.cos(ang)
    sa_ref[...] = jnp.where(lane >= ROPE_HALF, s, 0.0)
    sb_ref[...] = jnp.where(lane < ROPE_HALF, -s, 0.0)


def _rope_tables(positions):
    s = positions.shape[-1]
    inv_freq = ROPE_THETA ** (-jnp.arange(0, ROPE_DIM, 2, dtype=F32) / ROPE_DIM)
    invf = jnp.zeros((1, LANES), F32).at[0, :ROPE_DIM].set(jnp.concatenate([inv_freq, inv_freq]))
    tm = _tile(s, 2048, 8)
    tab = jax.ShapeDtypeStruct((s, LANES), F32)
    return pl.pallas_call(
        _rope_table_kernel,
        grid=(s // tm,),
        in_specs=[pl.BlockSpec((tm, 1), lambda i: (i, 0)), pl.BlockSpec((1, LANES), lambda i: (0, 0))],
        out_specs=[pl.BlockSpec((tm, LANES), lambda i: (i, 0))] * 3,
        out_shape=[tab] * 3,
        compiler_params=_params(("arbitrary",), 16 * tm * LANES * 4),
        name="rope_tables",
    )(positions.reshape(s, 1), invf)


def _norm_to_scratch(x_ref, g_ref, h_ref):
    @pl.when(pl.program_id(1) == 0)
    def _():
        h_ref[...] = _rms(x_ref[...], g_ref[...]).astype(h_ref.dtype)


def _norm_matmul_kernel(x_ref, g_ref, w_ref, o_ref, h_ref):
    _norm_to_scratch(x_ref, g_ref, h_ref)
    o_ref[...] = jnp.dot(h_ref[...], w_ref[...], preferred_element_type=F32).astype(o_ref.dtype)


def _norm_matmul(x, g, w, *, tm_pref=512, tn_pref=1024):
    m, d = x.shape
    n = w.shape[1]
    tm, tn = _tile(m, tm_pref, 8), _tile(n, tn_pref, LANES)
    vmem = 2 * tm * d * 4 + tm * d * 2 + 2 * d * tn * 2 + 2 * tm * tn * 2 + 2 * tm * tn * 4
    return pl.pallas_call(
        _norm_matmul_kernel,
        grid=(m // tm, n // tn),
        in_specs=[pl.BlockSpec((tm, d), lambda i, j: (i, 0)),
                  pl.BlockSpec((1, d), lambda i, j: (0, 0)),
                  pl.BlockSpec((d, tn), lambda i, j: (0, j))],
        out_specs=pl.BlockSpec((tm, tn), lambda i, j: (i, j)),
        out_shape=jax.ShapeDtypeStruct((m, n), BF16),
        scratch_shapes=[pltpu.VMEM((tm, d), BF16)],
        compiler_params=_params(("arbitrary", "arbitrary"), vmem),
        name="norm_matmul",
    )(x, g.reshape(1, d), w)


def _qkv_rope_kernel(x_ref, g_ref, w_ref, cos_ref, sa_ref, sb_ref, o_ref, h_ref, *, nq_blocks, nk_blocks, q_scale):
    _norm_to_scratch(x_ref, g_ref, h_ref)
    j = pl.program_id(1)
    y = jnp.dot(h_ref[...], w_ref[...], preferred_element_type=F32)
    tn = y.shape[1]

    def rope(scale):
        cos, sa, sb = cos_ref[...] * scale, sa_ref[...] * scale, sb_ref[...] * scale
        for c in range(tn // HEAD_DIM):
            sl = slice(c * HEAD_DIM, (c + 1) * HEAD_DIM)
            t = y[:, sl]
            r = t * cos + pltpu.roll(t, ROPE_HALF, 1) * sa + pltpu.roll(t, HEAD_DIM - ROPE_HALF, 1) * sb
            o_ref[:, sl] = r.astype(o_ref.dtype)

    @pl.when(j < nq_blocks)
    def _():
        rope(q_scale)

    @pl.when((j >= nq_blocks) & (j < nq_blocks + nk_blocks))
    def _():
        rope(1.0)

    @pl.when(j >= nq_blocks + nk_blocks)
    def _():
        o_ref[...] = y.astype(o_ref.dtype)


def _qkv_rope(x, g, w, tables, *, n_q, n_k, tm_pref=512):
    m, d = x.shape
    n = w.shape[1]
    tn = _tile(math.gcd(n_q, n_k, n - n_q - n_k), 512, HEAD_DIM)
    tm = _tile(m, tm_pref, 8)
    cos, sa, sb = tables
    tab_spec = pl.BlockSpec((tm, LANES), lambda i, j: (i, 0))
    vmem = 2 * tm * d * 4 + tm * d * 2 + 2 * d * tn * 2 + 2 * tm * tn * 2 + 3 * tm * tn * 4
    kern = functools.partial(_qkv_rope_kernel, nq_blocks=n_q // tn, nk_blocks=n_k // tn, q_scale=HEAD_DIM ** -0.5)
    return pl.pallas_call(
        kern,
        grid=(m // tm, n // tn),
        in_specs=[pl.BlockSpec((tm, d), lambda i, j: (i, 0)),
                  pl.BlockSpec((1, d), lambda i, j: (0, 0)),
                  pl.BlockSpec((d, tn), lambda i, j: (0, j)),
                  tab_spec, tab_spec, tab_spec],
        out_specs=pl.BlockSpec((tm, tn), lambda i, j: (i, j)),
        out_shape=jax.ShapeDtypeStruct((m, n), BF16),
        scratch_shapes=[pltpu.VMEM((tm, d), BF16)],
        compiler_params=_params(("arbitrary", "arbitrary"), vmem),
        name="qkv_rope",
    )(x, g.reshape(1, d), w, cos, sa, sb)


def _matmul_residual_kernel(a_ref, w_ref, x_ref, o_ref):
    o_ref[...] = x_ref[...] + jnp.dot(a_ref[...], w_ref[...], preferred_element_type=F32)


def _matmul_residual(a, w, x, *, tm_pref, tn_pref):
    m, kd = a.shape
    n = w.shape[1]
    tm, tn = _tile(m, tm_pref, 8), _tile(n, tn_pref, LANES)
    vmem = 2 * tm * kd * 2 + 2 * kd * tn * 2 + 5 * tm * tn * 4
    return pl.pallas_call(
        _matmul_residual_kernel,
        grid=(m // tm, n // tn),
        in_specs=[pl.BlockSpec((tm, kd), lambda i, j: (i, 0)),
                  pl.BlockSpec((kd, tn), lambda i, j: (0, j)),
                  pl.BlockSpec((tm, tn), lambda i, j: (i, j))],
        out_specs=pl.BlockSpec((tm, tn), lambda i, j: (i, j)),
        out_shape=jax.ShapeDtypeStruct((m, n), F32),
        compiler_params=_params(("arbitrary", "arbitrary"), vmem),
        name="matmul_residual",
    )(a, w, x)


def _swa_kernel(sink_ref, q_ref, kc_ref, kp_ref, vc_ref, vp_ref, o_ref, *, group):
    i, h = pl.program_id(0), pl.program_id(1)
    blk = SWA_BLOCK
    nsub = q_ref.shape[0] // blk
    row = lax.broadcasted_iota(jnp.int32, (blk, 2 * blk), 0)
    col = lax.broadcasted_iota(jnp.int32, (blk, 2 * blk), 1)
    rel = col - blk
    band = (rel <= row) & (row - rel < blk)
    nt = (((1,), (1,)), ((), ()))
    for c in range(nsub):
        rows = slice(c * blk, (c + 1) * blk)
        if c == 0:
            kprev, vprev = kp_ref[...], vp_ref[...]
            mask = band & (rel >= jnp.where(i > 0, -blk, 0))
        else:
            prev = slice((c - 1) * blk, c * blk)
            kprev, vprev = kc_ref[prev, :], vc_ref[prev, :]
            mask = band
        kw = jnp.concatenate([kprev, kc_ref[rows, :]], axis=0)
        vw = jnp.concatenate([vprev, vc_ref[rows, :]], axis=0)
        qs = jnp.concatenate([q_ref[rows, g * HEAD_DIM:(g + 1) * HEAD_DIM] for g in range(group)], axis=0)
        s = lax.dot_general(qs, kw, nt, preferred_element_type=F32)
        probs = []
        for g in range(group):
            sg = jnp.where(mask, s[g * blk:(g + 1) * blk], -jnp.inf)
            sink = sink_ref[h * group + g]
            mx = jnp.maximum(jnp.max(sg, axis=-1, keepdims=True), sink)
            p = jnp.exp(sg - mx)
            den = jnp.sum(p, axis=-1, keepdims=True) + jnp.exp(sink - mx)
            probs.append((p / den).astype(BF16))
        o = jnp.dot(jnp.concatenate(probs, axis=0), vw, preferred_element_type=F32)
        for g in range(group):
            o_ref[rows, g * HEAD_DIM:(g + 1) * HEAD_DIM] = o[g * blk:(g + 1) * blk].astype(o_ref.dtype)


def _swa_attention(qkv, sinks, *, hq, hkv, tq_pref=512):
    s = qkv.shape[0]
    group = hq // hkv
    tq = _tile(s, tq_pref, SWA_BLOCK)
    r = tq // SWA_BLOCK
    cur = lambda off: pl.BlockSpec((tq, HEAD_DIM), lambda i, h, sk: (i, off + h))
    prev = lambda off: pl.BlockSpec((SWA_BLOCK, HEAD_DIM), lambda i, h, sk: (jnp.maximum(i * r - 1, 0), off + h))
    vmem = 4 * tq * group * HEAD_DIM * 2 + 8 * tq * HEAD_DIM * 2 + 8 * group * SWA_BLOCK * 2 * SWA_BLOCK * 4
    return pl.pallas_call(
        functools.partial(_swa_kernel, group=group),
        grid_spec=pltpu.PrefetchScalarGridSpec(
            num_scalar_prefetch=1,
            grid=(s // tq, hkv),
            in_specs=[pl.BlockSpec((tq, group * HEAD_DIM), lambda i, h, sk: (i, h)),
                      cur(hq), prev(hq), cur(hq + hkv), prev(hq + hkv)],
            out_specs=pl.BlockSpec((tq, group * HEAD_DIM), lambda i, h, sk: (i, h)),
        ),
        out_shape=jax.ShapeDtypeStruct((s, hq * HEAD_DIM), BF16),
        compiler_params=_params(("arbitrary", "arbitrary"), vmem),
        name="swa_attention",
    )(sinks, qkv, qkv, qkv, qkv, qkv)


def _diff_kernel(lam_ref, subln_ref, q_ref, k_ref, v_ref, o_ref, qs_ref, m_ref, l_ref, acc_ref,
                 *, group, tk, lambda_init):
    qi = pl.program_id(1)
    tq = q_ref.shape[0]
    rows = group * tq
    nt = (((1,), (1,)), ((), ()))

    for mp in range(2):
        for g in range(group):
            c = (g * 2 + mp) * HEAD_DIM
            qs_ref[mp, g * tq:(g + 1) * tq, :] = q_ref[:, c:c + HEAD_DIM]
    m_ref[...] = jnp.full(m_ref.shape, -jnp.inf, F32)
    l_ref[...] = jnp.zeros(l_ref.shape, F32)
    acc_ref[...] = jnp.zeros(acc_ref.shape, F32)

    def step(kb, masked):
        start = pl.multiple_of(kb * tk, tk)
        kblk = k_ref[pl.ds(start, tk), :]
        vblk = v_ref[pl.ds(start, tk), :]
        if masked:
            qpos = qi * tq + (lax.broadcasted_iota(jnp.int32, (rows, tk), 0) & (tq - 1))
            kpos = start + lax.broadcasted_iota(jnp.int32, (rows, tk), 1)
            visible = kpos <= qpos
        for mp in range(2):
            s = lax.dot_general(qs_ref[mp], kblk[:, mp * HEAD_DIM:(mp + 1) * HEAD_DIM], nt,
                                preferred_element_type=F32)
            if masked:
                s = jnp.where(visible, s, -jnp.inf)
            m_prev = m_ref[mp]
            m_next = jnp.maximum(m_prev, jnp.max(s, axis=-1, keepdims=True))
            alpha = jnp.exp(m_prev - m_next)
            p = jnp.exp(s - pltpu.repeat(m_next, tk // LANES, 1))
            l_ref[mp] = alpha * l_ref[mp] + jnp.sum(p, axis=-1, keepdims=True)
            pv = jnp.dot(p.astype(BF16), vblk, preferred_element_type=F32)
            acc_ref[mp] = acc_ref[mp] * pltpu.repeat(alpha, acc_ref.shape[2] // LANES, 1) + pv
            m_ref[mp] = m_next

    n_full = (qi * tq) // tk

    def body(kb, carry):
        step(kb, False)
        return carry

    lax.fori_loop(0, n_full, body, 0)
    step(n_full, True)

    lam_vecs = lam_ref[...]
    lam = (jnp.exp(jnp.sum(lam_vecs[0:1] * lam_vecs[1:2], axis=-1, keepdims=True))
           - jnp.exp(jnp.sum(lam_vecs[2:3] * lam_vecs[3:4], axis=-1, keepdims=True)) + lambda_init)
    nrep = acc_ref.shape[2] // LANES
    o = (acc_ref[0] / pltpu.repeat(l_ref[0], nrep, 1)
         - lam * (acc_ref[1] / pltpu.repeat(l_ref[1], nrep, 1)))
    y = _rms(o, subln_ref[...]) * (1.0 - lambda_init)
    vd = y.shape[1]
    for g in range(group):
        o_ref[:, g * vd:(g + 1) * vd] = y[g * tq:(g + 1) * tq].astype(o_ref.dtype)


def _diff_attention(qkv, lam_vecs, subln, *, heads, hkv, lambda_init, tq_pref=256, tk_pref=512):
    s = qkv.shape[0]
    group = heads // hkv
    vd = 2 * HEAD_DIM
    tq = _tile(s, tq_pref, 8)
    tk = _tile(s, tk_pref, LANES)
    assert tk % tq == 0 and tq & (tq - 1) == 0, (tq, tk)
    rows = group * tq
    vmem = (4 * s * vd * 2 + 4 * tq * group * vd * 2 + 2 * rows * HEAD_DIM * 2
            + 4 * rows * LANES * 4 + 2 * rows * vd * 4 + 6 * rows * tk * 4)
    kern = functools.partial(_diff_kernel, group=group, tk=tk, lambda_init=lambda_init)
    return pl.pallas_call(
        kern,
        grid=(hkv, s // tq),
        in_specs=[pl.BlockSpec((4, HEAD_DIM), lambda h, i: (0, 0)),
                  pl.BlockSpec((1, vd), lambda h, i: (0, 0)),
                  pl.BlockSpec((tq, group * vd), lambda h, i: (i, h)),
                  pl.BlockSpec((s, vd), lambda h, i: (0, heads + h)),
                  pl.BlockSpec((s, vd), lambda h, i: (0, heads + hkv + h))],
        out_specs=pl.BlockSpec((tq, group * vd), lambda h, i: (i, h)),
        out_shape=jax.ShapeDtypeStruct((s, heads * vd), BF16),
        scratch_shapes=[pltpu.VMEM((2, rows, HEAD_DIM), BF16),
                        pltpu.VMEM((2, rows, LANES), F32),
                        pltpu.VMEM((2, rows, LANES), F32),
                        pltpu.VMEM((2, rows, vd), F32)],
        compiler_params=_params(("arbitrary", "arbitrary"), vmem),
        name="diff_attention",
    )(lam_vecs, subln.reshape(1, vd), qkv, qkv, qkv)


def _pool_kernel(x_ref, xp_ref, g_ref, w_ref, scale_ref, o_ref, ext_ref):
    i = pl.program_id(0)
    tm = x_ref.shape[0]
    ch = w_ref.shape[1]
    x = x_ref[...]
    h = _rms(x, g_ref[...])
    hp = _rms(xp_ref[...], g_ref[...])
    ext_ref[0:POOL_HALO, :] = jnp.where(i > 0, hp, 0.0)
    ext_ref[POOL_HALO:, :] = h
    t = i * tm + lax.broadcasted_iota(jnp.int32, (tm, 1), 0)
    for gi, win in enumerate(POOL_WINDOWS):
        cols = slice(gi * ch, (gi + 1) * ch)
        e = ext_ref[:, cols]
        span = 1
        while span < win:
            e = e + pltpu.roll(e, span, 0)
            span *= 2
        count = jnp.minimum(t + 1, win).astype(F32)
        pooled = (e[POOL_HALO:] / count - h[:, cols]).astype(BF16)
        y = jnp.dot(pooled, w_ref[gi], preferred_element_type=F32)
        o_ref[:, cols] = x[:, cols] + y * scale_ref[:, cols]


def _pool_mixer(x, g, w, scale, *, tm_pref=256):
    s, d = x.shape
    groups, ch, _ = w.shape
    assert groups == len(POOL_WINDOWS)
    tm = _tile(s, tm_pref, POOL_HALO)
    r = tm // POOL_HALO
    vec = pl.BlockSpec((1, d), lambda i: (0, 0))
    vmem = 4 * tm * d * 4 + 2 * groups * ch * ch * 2 + (tm + POOL_HALO) * d * 4 + 6 * tm * ch * 4
    return pl.pallas_call(
        _pool_kernel,
        grid=(s // tm,),
        in_specs=[pl.BlockSpec((tm, d), lambda i: (i, 0)),
                  pl.BlockSpec((POOL_HALO, d), lambda i: (jnp.maximum(i * r - 1, 0), 0)),
                  vec,
                  pl.BlockSpec((groups, ch, ch), lambda i: (0, 0, 0)),
                  vec],
        out_specs=pl.BlockSpec((tm, d), lambda i: (i, 0)),
        out_shape=jax.ShapeDtypeStruct((s, d), F32),
        scratch_shapes=[pltpu.VMEM((tm + POOL_HALO, d), F32)],
        compiler_params=_params(("arbitrary",), vmem),
        name="pool_mixer",
    )(x, x, g.reshape(1, d), w, scale.reshape(1, d))


def _xattn_kernel(x_ref, g_ref, wq_ref, kv_ref, wo_ref, o_ref, *, heads):
    x = x_ref[...]
    h = _rms(x, g_ref[...]).astype(BF16)
    width = wq_ref.shape[1]
    hd = width // heads
    q = (jnp.dot(h, wq_ref[...], preferred_element_type=F32) * (hd ** -0.5)).astype(BF16)
    nt = (((1,), (1,)), ((), ()))
    outs = []
    for a in range(heads):
        cols = slice(a * hd, (a + 1) * hd)
        s = lax.dot_general(q[:, cols], kv_ref[:, cols], nt, preferred_element_type=F32)
        p = jnp.exp(s - jnp.max(s, axis=-1, keepdims=True))
        den = jnp.sum(p, axis=-1, keepdims=True)
        pv = jnp.dot(p.astype(BF16), kv_ref[:, width + a * hd:width + (a + 1) * hd], preferred_element_type=F32)
        outs.append((pv / den).astype(BF16))
    o = jnp.concatenate(outs, axis=-1)
    o_ref[...] = x + jnp.dot(o, wo_ref[...], preferred_element_type=F32)


def _xattn(x, g, wq, kv, wo, *, tm_pref=256):
    s, d = x.shape
    width = wq.shape[1]
    mem_len = kv.shape[0]
    tm = _tile(s, tm_pref, 8)
    vmem = 4 * tm * d * 4 + 2 * d * width * 2 + mem_len * 2 * width * 2 + tm * d * 6 + 4 * tm * width * 4
    const = lambda shape: pl.BlockSpec(shape, lambda i: (0, 0), pipeline_mode=pl.Buffered(1))
    return pl.pallas_call(
        functools.partial(_xattn_kernel, heads=X_HEADS),
        grid=(s // tm,),
        in_specs=[pl.BlockSpec((tm, d), lambda i: (i, 0)), const((1, d)), const((d, width)),
                  const((mem_len, 2 * width)), const((width, d))],
        out_specs=pl.BlockSpec((tm, d), lambda i: (i, 0)),
        out_shape=jax.ShapeDtypeStruct((s, d), F32),
        compiler_params=_params(("arbitrary",), vmem),
        name="xattn",
    )(x, g.reshape(1, d), wq, kv, wo)


def _ffn_in_kernel(x_ref, g_ref, wg_ref, wv_ref, cwg_ref, cwv_ref, cbg_ref, cbv_ref, o_ref, h_ref, cg_ref, cv_ref):
    _norm_to_scratch(x_ref, g_ref, h_ref)
    i, j = pl.program_id(0), pl.program_id(1)
    tm = x_ref.shape[0]

    def conv(w_ref, cw_ref, cb_ref, carry_ref):
        u = jnp.dot(h_ref[...], w_ref[...], preferred_element_type=F32)
        prev = jnp.where(i > 0, carry_ref[j], 0.0)
        carry_ref[j] = u[tm - CONV_HALO:, :]
        ext = jnp.concatenate([prev, u], axis=0)
        c = cb_ref[...]
        for tap in range(CONV_WIDTH):
            lag = CONV_WIDTH - 1 - tap
            shifted = pltpu.roll(ext, lag, 0) if lag else ext
            c = c + cw_ref[tap:tap + 1, :] * shifted[CONV_HALO:, :]
        return c

    gate = conv(wg_ref, cwg_ref, cbg_ref, cg_ref)
    val = conv(wv_ref, cwv_ref, cbv_ref, cv_ref)
    o_ref[...] = (gate / (1.0 + jnp.exp(-gate)) * val).astype(o_ref.dtype)


def _ffn_in(x, g, w_in, conv_w, conv_b, *, tm_pref=512, tn_pref=512):
    m, d = x.shape
    f = w_in.shape[1] // 2
    tm, tn = _tile(m, tm_pref, 8), _tile(f, tn_pref, LANES)
    nj = f // tn
    lo = lambda shape: pl.BlockSpec(shape, lambda i, j: (0, j))
    hi = lambda shape: pl.BlockSpec(shape, lambda i, j: (0, j + nj))
    vmem = (2 * tm * d * 4 + tm * d * 2 + 4 * d * tn * 2 + 2 * tm * tn * 2 + 2 * nj * CONV_HALO * tn * 4
            + 8 * tm * tn * 4)
    return pl.pallas_call(
        _ffn_in_kernel,
        grid=(m // tm, nj),
        in_specs=[pl.BlockSpec((tm, d), lambda i, j: (i, 0)),
                  pl.BlockSpec((1, d), lambda i, j: (0, 0)),
                  lo((d, tn)), hi((d, tn)),
                  lo((CONV_WIDTH, tn)), hi((CONV_WIDTH, tn)),
                  lo((1, tn)), hi((1, tn))],
        out_specs=pl.BlockSpec((tm, tn), lambda i, j: (i, j)),
        out_shape=jax.ShapeDtypeStruct((m, f), BF16),
        scratch_shapes=[pltpu.VMEM((tm, d), BF16),
                        pltpu.VMEM((nj, CONV_HALO, tn), F32),
                        pltpu.VMEM((nj, CONV_HALO, tn), F32)],
        compiler_params=_params(("arbitrary", "arbitrary"), vmem),
        name="ffn_in",
    )(x, g.reshape(1, d), w_in, w_in, conv_w, conv_w, conv_b.reshape(1, -1), conv_b.reshape(1, -1))


def _final_norm_kernel(x_ref, g_ref, o_ref):
    o_ref[...] = _rms(x_ref[...], g_ref[...])


def _final_norm(x, g, *, tm_pref=512):
    s, d = x.shape
    tm = _tile(s, tm_pref, 8)
    return pl.pallas_call(
        _final_norm_kernel,
        grid=(s // tm,),
        in_specs=[pl.BlockSpec((tm, d), lambda i: (i, 0)), pl.BlockSpec((1, d), lambda i: (0, 0))],
        out_specs=pl.BlockSpec((tm, d), lambda i: (i, 0)),
        out_shape=jax.ShapeDtypeStruct((s, d), F32),
        compiler_params=_params(("arbitrary",), 6 * tm * d * 4),
        name="final_norm",
    )(x, g.reshape(1, d))


def kernel(x, mem, positions, swa_norm, swa_w_qkv, swa_sinks, swa_w_o, diff_norm, diff_w_qkv, diff_lambda_q1, diff_lambda_k1, diff_lambda_q2, diff_lambda_k2, diff_subln, diff_w_o, pool_norm, pool_w, pool_scale, xattn_norm, xattn_mem_norm, xattn_w_q, xattn_w_kv, xattn_w_o, ffn_norm, ffn_w_in, ffn_conv_w, ffn_conv_b, ffn_w_out, final_norm):
    batch, seq, d = x.shape
    assert batch == 1, "the kernels fold no batch dimension"
    depth = ffn_w_in.shape[0]
    xs = x.reshape(seq, d)
    mem2 = mem.reshape(mem.shape[1], d)
    tables = _rope_tables(positions)
    bf = lambda w: w.astype(BF16)

    swa_hq = d // HEAD_DIM
    swa_hkv = (swa_w_qkv.shape[-1] // HEAD_DIM - swa_hq) // 2
    diff_heads = d // (2 * HEAD_DIM)
    diff_hkv = (diff_w_qkv.shape[-1] // (2 * HEAD_DIM) - diff_heads) // 2

    for i in range(depth):
        kind, j = i % N_MIXERS, i // N_MIXERS
        if kind == 0:
            qkv = _qkv_rope(xs, swa_norm[j], bf(swa_w_qkv[j]), tables,
                            n_q=swa_hq * HEAD_DIM, n_k=swa_hkv * HEAD_DIM)
            att = _swa_attention(qkv, swa_sinks[j], hq=swa_hq, hkv=swa_hkv)
            xs = _matmul_residual(att, bf(swa_w_o[j]), xs, tm_pref=1024, tn_pref=512)
        elif kind == 1:
            lambda_init = 0.8 - 0.6 * math.exp(-0.3 * i)
            qkv = _qkv_rope(xs, diff_norm[j], bf(diff_w_qkv[j]), tables,
                            n_q=2 * diff_heads * HEAD_DIM, n_k=2 * diff_hkv * HEAD_DIM)
            lam_vecs = jnp.stack([diff_lambda_q1[j], diff_lambda_k1[j], diff_lambda_q2[j], diff_lambda_k2[j]])
            att = _diff_attention(qkv, lam_vecs, diff_subln[j], heads=diff_heads, hkv=diff_hkv,
                                  lambda_init=lambda_init)
            xs = _matmul_residual(att, bf(diff_w_o[j]), xs, tm_pref=1024, tn_pref=512)
        else:
            xs = _pool_mixer(xs, pool_norm[j], bf(pool_w[j]), pool_scale[j])
        kv = _norm_matmul(mem2, xattn_mem_norm[i], bf(xattn_w_kv[i]))
        xs = _xattn(xs, xattn_norm[i], bf(xattn_w_q[i]), kv, bf(xattn_w_o[i]))
        act = _ffn_in(xs, ffn_norm[i], bf(ffn_w_in[i]), ffn_conv_w[i], ffn_conv_b[i])
        xs = _matmul_residual(act, bf(ffn_w_out[i]), xs, tm_pref=512, tn_pref=256)
    return _final_norm(xs, final_norm).reshape(batch, seq, d)
```

```python
import functools
import math

import jax
import jax.numpy as jnp
from jax import lax
from jax.experimental import pallas as pl
from jax.experimental.pallas import tpu as pltpu

HEAD_DIM = 128
ROPE_DIM = HEAD_DIM // 4
ROPE_HALF = ROPE_DIM // 2
ROPE_THETA = 500000.0
SWA_BLOCK = 128
POOL_WINDOWS = (2, 4, 8, 16)
POOL_HALO = 16
X_HEADS = 4
CONV_WIDTH = 3
CONV_HALO = 8
N_MIXERS = 3
NORM_EPS = 1e-5
LANES = 128
LOG2E = math.log2(math.e)

V7X_VMEM_BYTES = 64 * 2**20
VMEM_RESERVE_BYTES = 6 * 2**20

F32 = jnp.float32
BF16 = jnp.bfloat16

ROPE_Q_SWA, ROPE_Q_DIFF, ROPE_K, ROPE_NONE = 0, 1, 2, 3
ROPE_SCALES = (HEAD_DIM ** -0.5, HEAD_DIM ** -0.5 * LOG2E, 1.0)


def _params(semantics, vmem_bytes, flags=None):
    limit = min(int(vmem_bytes) * 5 // 4 + VMEM_RESERVE_BYTES, V7X_VMEM_BYTES - VMEM_RESERVE_BYTES // 2)
    return pltpu.CompilerParams(dimension_semantics=semantics, vmem_limit_bytes=limit, flags=flags)


def _tile(dim, pref, align):
    if dim <= pref:
        return dim
    t = (pref // align) * align
    while t > align and dim % t:
        t -= align
    assert dim % t == 0, (dim, pref, align)
    return t


def _rms(x, g, eps=NORM_EPS):
    ms = jnp.mean(x * x, axis=-1, keepdims=True)
    return x * lax.rsqrt(ms + eps) * g


def _row(stack):
    return stack.reshape(stack.shape[0], 1, stack.shape[-1])


def _rope_table_kernel(pos_ref, invf_ref, cos_ref, sa_ref, sb_ref):
    ang = pos_ref[...].astype(F32) * invf_ref[...]
    lane = lax.broadcasted_iota(jnp.int32, ang.shape, 1)
    s = jnp.sin(ang)
    cos = jnp.cos(ang)
    sa = jnp.where(lane >= ROPE_HALF, s, 0.0)
    sb = jnp.where(lane < ROPE_HALF, -s, 0.0)
    for v, scale in enumerate(ROPE_SCALES):
        cos_ref[v], sa_ref[v], sb_ref[v] = cos * scale, sa * scale, sb * scale
    cos_ref[ROPE_NONE] = jnp.ones(cos.shape, F32)
    sa_ref[ROPE_NONE] = jnp.zeros(cos.shape, F32)
    sb_ref[ROPE_NONE] = jnp.zeros(cos.shape, F32)


def _rope_tables(positions):
    s = positions.shape[-1]
    inv_freq = ROPE_THETA ** (-jnp.arange(0, ROPE_DIM, 2, dtype=F32) / ROPE_DIM)
    invf = jnp.zeros((1, LANES), F32).at[0, :ROPE_DIM].set(jnp.concatenate([inv_freq, inv_freq]))
    nv = len(ROPE_SCALES) + 1
    tm = _tile(s, 1024, 8)
    tab = jax.ShapeDtypeStruct((nv, s, LANES), F32)
    return pl.pallas_call(
        _rope_table_kernel,
        grid=(s // tm,),
        in_specs=[pl.BlockSpec((tm, 1), lambda i: (i, 0)), pl.BlockSpec((1, LANES), lambda i: (0, 0))],
        out_specs=[pl.BlockSpec((nv, tm, LANES), lambda i: (0, i, 0))] * 3,
        out_shape=[tab] * 3,
        compiler_params=_params(("arbitrary",), (6 * nv + 8) * tm * LANES * 4),
        name="rope_tables",
    )(positions.reshape(s, 1), invf)


def _norm_to_scratch(x_ref, g_ref, h_ref):
    @pl.when(pl.program_id(1) == 0)
    def _():
        h_ref[...] = _rms(x_ref[...], g_ref[...]).astype(h_ref.dtype)


def _norm_matmul_kernel(x_ref, g_ref, w_ref, o_ref, h_ref):
    _norm_to_scratch(x_ref, g_ref, h_ref)
    o_ref[...] = jnp.dot(h_ref[...], w_ref[...], preferred_element_type=F32).astype(o_ref.dtype)


def _norm_matmul(x, g, w, li, *, tm_pref=512, tn_pref=1024):
    m, d = x.shape
    n = w.shape[-1]
    tm, tn = _tile(m, tm_pref, 8), _tile(n, tn_pref, LANES)
    vmem = 2 * tm * d * 4 + tm * d * 2 + 2 * d * tn * 2 + 2 * tm * tn * 2 + 2 * tm * tn * 4
    return pl.pallas_call(
        _norm_matmul_kernel,
        grid=(m // tm, n // tn),
        in_specs=[pl.BlockSpec((tm, d), lambda i, j: (i, 0)),
                  pl.BlockSpec((None, 1, d), lambda i, j: (li, 0, 0)),
                  pl.BlockSpec((None, d, tn), lambda i, j: (li, 0, j))],
        out_specs=pl.BlockSpec((tm, tn), lambda i, j: (i, j)),
        out_shape=jax.ShapeDtypeStruct((m, n), BF16),
        scratch_shapes=[pltpu.VMEM((tm, d), BF16)],
        compiler_params=_params(("arbitrary", "arbitrary"), vmem),
        name="norm_matmul",
    )(x, _row(g), w)


def _qkv_rope_kernel(x_ref, g_ref, w_ref, cos_ref, sa_ref, sb_ref, o_ref, h_ref):
    _norm_to_scratch(x_ref, g_ref, h_ref)
    y = jnp.dot(h_ref[...], w_ref[...], preferred_element_type=F32)
    cos, sa, sb = cos_ref[...], sa_ref[...], sb_ref[...]
    for c in range(y.shape[1] // HEAD_DIM):
        sl = slice(c * HEAD_DIM, (c + 1) * HEAD_DIM)
        t = y[:, sl]
        r = t * cos + pltpu.roll(t, ROPE_HALF, 1) * sa + pltpu.roll(t, HEAD_DIM - ROPE_HALF, 1) * sb
        o_ref[:, sl] = r.astype(o_ref.dtype)


def _qkv_rope(x, g, w, li, tables, *, n_q, n_k, q_variant, tm_pref=512):
    m, d = x.shape
    n = w.shape[-1]
    tn = _tile(math.gcd(n_q, n_k, n - n_q - n_k), 512, HEAD_DIM)
    tm = _tile(m, tm_pref, 8)
    nqb, nkb = n_q // tn, n_k // tn

    def tab_index(i, j):
        return (jnp.where(j < nqb, q_variant, jnp.where(j < nqb + nkb, ROPE_K, ROPE_NONE)), i, 0)

    tab_spec = pl.BlockSpec((None, tm, LANES), tab_index)
    vmem = 2 * tm * d * 4 + tm * d * 2 + 2 * d * tn * 2 + 2 * tm * tn * 2 + 3 * tm * tn * 4
    return pl.pallas_call(
        _qkv_rope_kernel,
        grid=(m // tm, n // tn),
        in_specs=[pl.BlockSpec((tm, d), lambda i, j: (i, 0)),
                  pl.BlockSpec((None, 1, d), lambda i, j: (li, 0, 0)),
                  pl.BlockSpec((None, d, tn), lambda i, j: (li, 0, j)),
                  tab_spec, tab_spec, tab_spec],
        out_specs=pl.BlockSpec((tm, tn), lambda i, j: (i, j)),
        out_shape=jax.ShapeDtypeStruct((m, n), BF16),
        scratch_shapes=[pltpu.VMEM((tm, d), BF16)],
        compiler_params=_params(("arbitrary", "arbitrary"), vmem),
        name="qkv_rope",
    )(x, _row(g), w, *tables)


def _matmul_residual_kernel(a_ref, w_ref, x_ref, o_ref):
    o_ref[...] = x_ref[...] + jnp.dot(a_ref[...], w_ref[...], preferred_element_type=F32)


def _matmul_residual(a, w, li, x, *, tm_pref, tn_pref):
    m, kd = a.shape
    n = w.shape[-1]
    tm, tn = _tile(m, tm_pref, 8), _tile(n, tn_pref, LANES)
    vmem = 2 * tm * kd * 2 + 2 * kd * tn * 2 + 5 * tm * tn * 4
    return pl.pallas_call(
        _matmul_residual_kernel,
        grid=(m // tm, n // tn),
        in_specs=[pl.BlockSpec((tm, kd), lambda i, j: (i, 0)),
                  pl.BlockSpec((None, kd, tn), lambda i, j: (li, 0, j)),
                  pl.BlockSpec((tm, tn), lambda i, j: (i, j))],
        out_specs=pl.BlockSpec((tm, tn), lambda i, j: (i, j)),
        out_shape=jax.ShapeDtypeStruct((m, n), F32),
        compiler_params=_params(("arbitrary", "arbitrary"), vmem),
        name="matmul_residual",
    )(a, w, x)


def _swa_kernel(sink_ref, q_ref, kc_ref, kp_ref, vc_ref, vp_ref, o_ref, *, group):
    i, h = pl.program_id(0), pl.program_id(1)
    blk = SWA_BLOCK
    nsub = q_ref.shape[0] // blk
    row = lax.broadcasted_iota(jnp.int32, (blk, 2 * blk), 0)
    col = lax.broadcasted_iota(jnp.int32, (blk, 2 * blk), 1)
    rel = col - blk
    band = (rel <= row) & (row - rel < blk)
    nt = (((1,), (1,)), ((), ()))
    for c in range(nsub):
        rows = slice(c * blk, (c + 1) * blk)
        if c == 0:
            kprev, vprev = kp_ref[...], vp_ref[...]
            mask = band & (rel >= jnp.where(i > 0, -blk, 0))
        else:
            prev = slice((c - 1) * blk, c * blk)
            kprev, vprev = kc_ref[prev, :], vc_ref[prev, :]
            mask = band
        kw = jnp.concatenate([kprev, kc_ref[rows, :]], axis=0)
        vw = jnp.concatenate([vprev, vc_ref[rows, :]], axis=0)
        qs = jnp.concatenate([q_ref[rows, g * HEAD_DIM:(g + 1) * HEAD_DIM] for g in range(group)], axis=0)
        s = lax.dot_general(qs, kw, nt, preferred_element_type=F32)
        probs = []
        for g in range(group):
            sg = jnp.where(mask, s[g * blk:(g + 1) * blk], -jnp.inf)
            sink = sink_ref[h * group + g]
            mx = jnp.maximum(jnp.max(sg, axis=-1, keepdims=True), sink)
            p = jnp.exp(sg - mx)
            den = jnp.sum(p, axis=-1, keepdims=True) + jnp.exp(sink - mx)
            probs.append((p / den).astype(BF16))
        o = jnp.dot(jnp.concatenate(probs, axis=0), vw, preferred_element_type=F32)
        for g in range(group):
            o_ref[rows, g * HEAD_DIM:(g + 1) * HEAD_DIM] = o[g * blk:(g + 1) * blk].astype(o_ref.dtype)


def _swa_attention(qkv, sinks, *, hq, hkv, tq_pref=512):
    s = qkv.shape[0]
    group = hq // hkv
    tq = _tile(s, tq_pref, SWA_BLOCK)
    r = tq // SWA_BLOCK
    cur = lambda off: pl.BlockSpec((tq, HEAD_DIM), lambda i, h, sk: (i, off + h))
    prev = lambda off: pl.BlockSpec((SWA_BLOCK, HEAD_DIM), lambda i, h, sk: (jnp.maximum(i * r - 1, 0), off + h))
    vmem = 4 * tq * group * HEAD_DIM * 2 + 8 * tq * HEAD_DIM * 2 + 8 * group * SWA_BLOCK * 2 * SWA_BLOCK * 4
    return pl.pallas_call(
        functools.partial(_swa_kernel, group=group),
        grid_spec=pltpu.PrefetchScalarGridSpec(
            num_scalar_prefetch=1,
            grid=(s // tq, hkv),
            in_specs=[pl.BlockSpec((tq, group * HEAD_DIM), lambda i, h, sk: (i, h)),
                      cur(hq), prev(hq), cur(hq + hkv), prev(hq + hkv)],
            out_specs=pl.BlockSpec((tq, group * HEAD_DIM), lambda i, h, sk: (i, h)),
        ),
        out_shape=jax.ShapeDtypeStruct((s, hq * HEAD_DIM), BF16),
        compiler_params=_params(("arbitrary", "arbitrary"), vmem),
        name="swa_attention",
    )(sinks, qkv, qkv, qkv, qkv, qkv)


def _diff_kernel(lam_ref, subln_ref, q_ref, k_ref, v_ref, o_ref, qs_ref, m_ref, l_ref, acc_ref,
                 *, group, tk, lambda_init):
    qi = pl.program_id(1)
    tq = q_ref.shape[0]
    rows = group * tq
    nt = (((1,), (1,)), ((), ()))

    for mp in range(2):
        for g in range(group):
            c = (g * 2 + mp) * HEAD_DIM
            qs_ref[mp, g * tq:(g + 1) * tq, :] = q_ref[:, c:c + HEAD_DIM]
    m_ref[...] = jnp.full(m_ref.shape, -jnp.inf, F32)
    l_ref[...] = jnp.zeros(l_ref.shape, F32)
    acc_ref[...] = jnp.zeros(acc_ref.shape, F32)

    def step(kb, masked):
        start = pl.multiple_of(kb * tk, tk)
        kblk = k_ref[pl.ds(start, tk), :]
        vblk = v_ref[pl.ds(start, tk), :]
        if masked:
            qpos = qi * tq + (lax.broadcasted_iota(jnp.int32, (rows, tk), 0) & (tq - 1))
            kpos = start + lax.broadcasted_iota(jnp.int32, (rows, tk), 1)
            visible = kpos <= qpos
        for mp in range(2):
            s = lax.dot_general(qs_ref[mp], kblk[:, mp * HEAD_DIM:(mp + 1) * HEAD_DIM], nt,
                                preferred_element_type=F32)
            if masked:
                s = jnp.where(visible, s, -jnp.inf)
            m_prev = m_ref[mp]
            m_next = jnp.maximum(m_prev, jnp.max(s, axis=-1, keepdims=True))
            alpha = jnp.exp2(m_prev - m_next)
            p = jnp.exp2(s - pltpu.repeat(m_next, tk // LANES, 1))
            l_ref[mp] = alpha * l_ref[mp] + jnp.sum(p, axis=-1, keepdims=True)
            pv = jnp.dot(p.astype(BF16), vblk, preferred_element_type=F32)
            acc_ref[mp] = acc_ref[mp] * pltpu.repeat(alpha, acc_ref.shape[2] // LANES, 1) + pv
            m_ref[mp] = m_next

    n_full = (qi * tq) // tk

    def body(kb, carry):
        step(kb, False)
        return carry

    lax.fori_loop(0, n_full, body, 0)
    step(n_full, True)

    lam_vecs = lam_ref[...]
    lam = (jnp.exp(jnp.sum(lam_vecs[0:1] * lam_vecs[1:2], axis=-1, keepdims=True))
           - jnp.exp(jnp.sum(lam_vecs[2:3] * lam_vecs[3:4], axis=-1, keepdims=True)) + lambda_init)
    nrep = acc_ref.shape[2] // LANES
    o = (acc_ref[0] / pltpu.repeat(l_ref[0], nrep, 1)
         - lam * (acc_ref[1] / pltpu.repeat(l_ref[1], nrep, 1)))
    y = _rms(o, subln_ref[...]) * (1.0 - lambda_init)
    vd = y.shape[1]
    for g in range(group):
        o_ref[:, g * vd:(g + 1) * vd] = y[g * tq:(g + 1) * tq].astype(o_ref.dtype)


def _diff_attention(qkv, lam_vecs, subln, li, *, heads, hkv, lambda_init, tq_pref=512, tk_pref=512):
    s = qkv.shape[0]
    group = heads // hkv
    vd = 2 * HEAD_DIM
    tq = _tile(s, tq_pref, 8)
    tk = _tile(s, tk_pref, LANES)
    assert tk % tq == 0 and tq & (tq - 1) == 0, (tq, tk)
    rows = group * tq
    vmem = (2 * s * vd * 2 + 4 * tq * group * vd * 2 + 2 * rows * HEAD_DIM * 2
            + 4 * rows * LANES * 4 + 2 * rows * vd * 4 + 6 * rows * tk * 4)
    kern = functools.partial(_diff_kernel, group=group, tk=tk, lambda_init=lambda_init)
    resident = lambda off: pl.BlockSpec((s, vd), lambda h, i: (0, off + h), pipeline_mode=pl.Buffered(1))
    return pl.pallas_call(
        kern,
        grid=(hkv, s // tq),
        in_specs=[pl.BlockSpec((4, HEAD_DIM), lambda h, i: (0, 0)),
                  pl.BlockSpec((None, 1, vd), lambda h, i: (li, 0, 0)),
                  pl.BlockSpec((tq, group * vd), lambda h, i: (i, h)),
                  resident(heads), resident(heads + hkv)],
        out_specs=pl.BlockSpec((tq, group * vd), lambda h, i: (i, h)),
        out_shape=jax.ShapeDtypeStruct((s, heads * vd), BF16),
        scratch_shapes=[pltpu.VMEM((2, rows, HEAD_DIM), BF16),
                        pltpu.VMEM((2, rows, LANES), F32),
                        pltpu.VMEM((2, rows, LANES), F32),
                        pltpu.VMEM((2, rows, vd), F32)],
        compiler_params=_params(("arbitrary", "arbitrary"), vmem),
        name="diff_attention",
    )(lam_vecs, _row(subln), qkv, qkv, qkv)


def _pool_kernel(x_ref, xp_ref, g_ref, w_ref, scale_ref, o_ref, ext_ref):
    i = pl.program_id(0)
    tm = x_ref.shape[0]
    ch = w_ref.shape[1]
    x = x_ref[...]
    h = _rms(x, g_ref[...])
    hp = _rms(xp_ref[...], g_ref[...])
    ext_ref[0:POOL_HALO, :] = jnp.where(i > 0, hp, 0.0)
    ext_ref[POOL_HALO:, :] = h
    t = i * tm + lax.broadcasted_iota(jnp.int32, (tm, 1), 0)
    for gi, win in enumerate(POOL_WINDOWS):
        cols = slice(gi * ch, (gi + 1) * ch)
        e = ext_ref[:, cols]
        span = 1
        while span < win:
            e = e + pltpu.roll(e, span, 0)
            span *= 2
        count = jnp.minimum(t + 1, win).astype(F32)
        pooled = (e[POOL_HALO:] / count - h[:, cols]).astype(BF16)
        y = jnp.dot(pooled, w_ref[gi], preferred_element_type=F32)
        o_ref[:, cols] = x[:, cols] + y * scale_ref[:, cols]


def _pool_mixer(x, g, w, scale, li, *, tm_pref=256):
    s, d = x.shape
    groups, ch = w.shape[1], w.shape[2]
    assert groups == len(POOL_WINDOWS)
    tm = _tile(s, tm_pref, POOL_HALO)
    r = tm // POOL_HALO
    vec = pl.BlockSpec((None, 1, d), lambda i: (li, 0, 0))
    vmem = 4 * tm * d * 4 + 2 * groups * ch * ch * 2 + (tm + POOL_HALO) * d * 4 + 6 * tm * ch * 4
    return pl.pallas_call(
        _pool_kernel,
        grid=(s // tm,),
        in_specs=[pl.BlockSpec((tm, d), lambda i: (i, 0)),
                  pl.BlockSpec((POOL_HALO, d), lambda i: (jnp.maximum(i * r - 1, 0), 0)),
                  vec,
                  pl.BlockSpec((None, groups, ch, ch), lambda i: (li, 0, 0, 0)),
                  vec],
        out_specs=pl.BlockSpec((tm, d), lambda i: (i, 0)),
        out_shape=jax.ShapeDtypeStruct((s, d), F32),
        scratch_shapes=[pltpu.VMEM((tm + POOL_HALO, d), F32)],
        compiler_params=_params(("arbitrary",), vmem),
        name="pool_mixer",
    )(x, x, _row(g), w, _row(scale))


def _xattn_kernel(x_ref, g_ref, wq_ref, kv_ref, wo_ref, gn_ref, o_ref, hn_ref, *, heads):
    x = x_ref[...]
    h = _rms(x, g_ref[...]).astype(BF16)
    width = wq_ref.shape[1]
    hd = width // heads
    q = (jnp.dot(h, wq_ref[...], preferred_element_type=F32) * (hd ** -0.5)).astype(BF16)
    nt = (((1,), (1,)), ((), ()))
    outs = []
    for a in range(heads):
        cols = slice(a * hd, (a + 1) * hd)
        s = lax.dot_general(q[:, cols], kv_ref[:, cols], nt, preferred_element_type=F32)
        p = jnp.exp(s - jnp.max(s, axis=-1, keepdims=True))
        den = jnp.sum(p, axis=-1, keepdims=True)
        pv = jnp.dot(p.astype(BF16), kv_ref[:, width + a * hd:width + (a + 1) * hd], preferred_element_type=F32)
        outs.append((pv / den).astype(BF16))
    o = jnp.concatenate(outs, axis=-1)
    y = x + jnp.dot(o, wo_ref[...], preferred_element_type=F32)
    o_ref[...] = y
    hn_ref[...] = _rms(y, gn_ref[...]).astype(hn_ref.dtype)


def _xattn(x, g, wq, kv, wo, g_next, li, *, tm_pref=256):
    s, d = x.shape
    width = wq.shape[-1]
    mem_len = kv.shape[0]
    tm = _tile(s, tm_pref, 8)
    vmem = (4 * tm * d * 4 + 2 * tm * d * 2 + 2 * d * width * 2 + mem_len * 2 * width * 2 + tm * d * 6
            + 4 * tm * width * 4)
    layer = lambda shape: pl.BlockSpec((None,) + shape, lambda i: (li, 0, 0), pipeline_mode=pl.Buffered(1))
    rows = pl.BlockSpec((tm, d), lambda i: (i, 0))
    return pl.pallas_call(
        functools.partial(_xattn_kernel, heads=X_HEADS),
        grid=(s // tm,),
        in_specs=[rows, layer((1, d)), layer((d, width)),
                  pl.BlockSpec((mem_len, 2 * width), lambda i: (0, 0), pipeline_mode=pl.Buffered(1)),
                  layer((width, d)), layer((1, d))],
        out_specs=[rows, rows],
        out_shape=[jax.ShapeDtypeStruct((s, d), F32), jax.ShapeDtypeStruct((s, d), BF16)],
        compiler_params=_params(("arbitrary",), vmem),
        name="xattn",
    )(x, _row(g), wq, kv, wo, _row(g_next))


def _ffn_in_kernel(h_ref, wg_ref, wv_ref, cwg_ref, cwv_ref, cbg_ref, cbv_ref, o_ref, cg_ref, cv_ref, *, chunk):
    i, j = pl.program_id(0), pl.program_id(1)
    tm = h_ref.shape[0]
    h = h_ref[...]

    def conv(w_ref, cw_ref, cb_ref, carry_ref):
        u = jnp.dot(h, w_ref[...], preferred_element_type=F32)
        first = jnp.where(i > 0, carry_ref[j], 0.0)
        carry_ref[j] = u[tm - CONV_HALO:, :]
        out = []
        for r0 in range(0, tm, chunk):
            prev = first if r0 == 0 else u[r0 - CONV_HALO:r0, :]
            ext = jnp.concatenate([prev, u[r0:r0 + chunk, :]], axis=0)
            c = cb_ref[...]
            for tap in range(CONV_WIDTH):
                lag = CONV_WIDTH - 1 - tap
                shifted = pltpu.roll(ext, lag, 0) if lag else ext
                c = c + cw_ref[tap:tap + 1, :] * shifted[CONV_HALO:, :]
            out.append(c)
        return out

    gate = conv(wg_ref, cwg_ref, cbg_ref, cg_ref)
    val = conv(wv_ref, cwv_ref, cbv_ref, cv_ref)
    for k, (gk, vk) in enumerate(zip(gate, val)):
        o_ref[k * chunk:(k + 1) * chunk, :] = (gk / (1.0 + jnp.exp(-gk)) * vk).astype(o_ref.dtype)


def _ffn_in(h, w_in, conv_w, conv_b, li, *, tm_pref=1024, tn_pref=512, chunk_pref=256):
    m, d = h.shape
    f = w_in.shape[-1] // 2
    tm, tn = _tile(m, tm_pref, 8), _tile(f, tn_pref, LANES)
    chunk = _tile(tm, chunk_pref, 8)
    nj = f // tn
    lo = lambda rows: pl.BlockSpec((None, rows, tn), lambda i, j: (li, 0, j))
    hi = lambda rows: pl.BlockSpec((None, rows, tn), lambda i, j: (li, 0, j + nj))
    vmem = 2 * tm * d * 2 + 4 * d * tn * 2 + 2 * tm * tn * 2 + 2 * nj * CONV_HALO * tn * 4 + 6 * tm * tn * 4
    return pl.pallas_call(
        functools.partial(_ffn_in_kernel, chunk=chunk),
        grid=(m // tm, nj),
        in_specs=[pl.BlockSpec((tm, d), lambda i, j: (i, 0)),
                  lo(d), hi(d), lo(CONV_WIDTH), hi(CONV_WIDTH), lo(1), hi(1)],
        out_specs=pl.BlockSpec((tm, tn), lambda i, j: (i, j)),
        out_shape=jax.ShapeDtypeStruct((m, f), BF16),
        scratch_shapes=[pltpu.VMEM((nj, CONV_HALO, tn), F32),
                        pltpu.VMEM((nj, CONV_HALO, tn), F32)],
        compiler_params=_params(("arbitrary", "arbitrary"), vmem),
        name="ffn_in",
    )(h, w_in, w_in, conv_w, conv_w, _row(conv_b), _row(conv_b))


def _final_norm_kernel(x_ref, g_ref, o_ref):
    o_ref[...] = _rms(x_ref[...], g_ref[...])


def _final_norm(x, g, *, tm_pref=512):
    s, d = x.shape
    tm = _tile(s, tm_pref, 8)
    return pl.pallas_call(
        _final_norm_kernel,
        grid=(s // tm,),
        in_specs=[pl.BlockSpec((tm, d), lambda i: (i, 0)), pl.BlockSpec((1, d), lambda i: (0, 0))],
        out_specs=pl.BlockSpec((tm, d), lambda i: (i, 0)),
        out_shape=jax.ShapeDtypeStruct((s, d), F32),
        compiler_params=_params(("arbitrary",), 6 * tm * d * 4),
        name="final_norm",
    )(x, g.reshape(1, d))


def kernel(x, mem, positions, swa_norm, swa_w_qkv, swa_sinks, swa_w_o, diff_norm, diff_w_qkv, diff_lambda_q1, diff_lambda_k1, diff_lambda_q2, diff_lambda_k2, diff_subln, diff_w_o, pool_norm, pool_w, pool_scale, xattn_norm, xattn_mem_norm, xattn_w_q, xattn_w_kv, xattn_w_o, ffn_norm, ffn_w_in, ffn_conv_w, ffn_conv_b, ffn_w_out, final_norm):
    batch, seq, d = x.shape
    assert batch == 1, "the kernels fold no batch dimension"
    depth = ffn_w_in.shape[0]
    xs = x.reshape(seq, d)
    mem2 = mem.reshape(mem.shape[1], d)
    tables = _rope_tables(positions)
    bf = lambda w: w.astype(BF16)
    swa_w_qkv, swa_w_o, diff_w_qkv, diff_w_o, pool_w = map(bf, (swa_w_qkv, swa_w_o, diff_w_qkv, diff_w_o, pool_w))
    xattn_w_q, xattn_w_kv, xattn_w_o, ffn_w_in, ffn_w_out = map(
        bf, (xattn_w_q, xattn_w_kv, xattn_w_o, ffn_w_in, ffn_w_out))

    swa_hq = d // HEAD_DIM
    swa_hkv = (swa_w_qkv.shape[-1] // HEAD_DIM - swa_hq) // 2
    diff_heads = d // (2 * HEAD_DIM)
    diff_hkv = (diff_w_qkv.shape[-1] // (2 * HEAD_DIM) - diff_heads) // 2

    for i in range(depth):
        kind, j = i % N_MIXERS, i // N_MIXERS
        if kind == 0:
            qkv = _qkv_rope(xs, swa_norm, swa_w_qkv, j, tables, n_q=swa_hq * HEAD_DIM, n_k=swa_hkv * HEAD_DIM,
                            q_variant=ROPE_Q_SWA)
            att = _swa_attention(qkv, swa_sinks[j], hq=swa_hq, hkv=swa_hkv)
            xs = _matmul_residual(att, swa_w_o, j, xs, tm_pref=1024, tn_pref=512)
        elif kind == 1:
            lambda_init = 0.8 - 0.6 * math.exp(-0.3 * i)
            qkv = _qkv_rope(xs, diff_norm, diff_w_qkv, j, tables, n_q=2 * diff_heads * HEAD_DIM,
                            n_k=2 * diff_hkv * HEAD_DIM, q_variant=ROPE_Q_DIFF)
            lam_vecs = jnp.stack([diff_lambda_q1[j], diff_lambda_k1[j], diff_lambda_q2[j], diff_lambda_k2[j]])
            att = _diff_attention(qkv, lam_vecs, diff_subln, j, heads=diff_heads, hkv=diff_hkv,
                                  lambda_init=lambda_init)
            xs = _matmul_residual(att, diff_w_o, j, xs, tm_pref=1024, tn_pref=512)
        else:
            xs = _pool_mixer(xs, pool_norm, pool_w, pool_scale, j)
        kv = _norm_matmul(mem2, xattn_mem_norm, xattn_w_kv, i)
        xs, hn = _xattn(xs, xattn_norm, xattn_w_q, kv, xattn_w_o, ffn_norm, i)
        act = _ffn_in(hn, ffn_w_in, ffn_conv_w, ffn_conv_b, i)
        xs = _matmul_residual(act, ffn_w_out, i, xs, tm_pref=512, tn_pref=256)
    return _final_norm(xs, final_norm).reshape(batch, seq, d)
```

```python
import functools
import math

import jax
import jax.numpy as jnp
from jax import lax
from jax.experimental import pallas as pl
from jax.experimental.pallas import tpu as pltpu

HEAD_DIM = 128
ROPE_DIM = HEAD_DIM // 4
ROPE_HALF = ROPE_DIM // 2
ROPE_THETA = 500000.0
SWA_BLOCK = 128
POOL_WINDOWS = (2, 4, 8, 16)
POOL_HALO = 16
X_HEADS = 4
CONV_WIDTH = 3
CONV_HALO = 8
N_MIXERS = 3
NORM_EPS = 1e-5
LANES = 128
LOG2E = math.log2(math.e)

V7X_VMEM_BYTES = 64 * 2**20
VMEM_RESERVE_BYTES = 6 * 2**20

F32 = jnp.float32
BF16 = jnp.bfloat16

ROPE_Q_SWA, ROPE_Q_DIFF, ROPE_K, ROPE_NONE = 0, 1, 2, 3
ROPE_SCALES = (HEAD_DIM ** -0.5, HEAD_DIM ** -0.5 * LOG2E, 1.0)


def _params(semantics, vmem_bytes, flags=None):
    limit = min(int(vmem_bytes) * 5 // 4 + VMEM_RESERVE_BYTES, V7X_VMEM_BYTES - VMEM_RESERVE_BYTES // 2)
    return pltpu.CompilerParams(dimension_semantics=semantics, vmem_limit_bytes=limit, flags=flags)


def _tile(dim, pref, align):
    if dim <= pref:
        return dim
    t = (pref // align) * align
    while t > align and dim % t:
        t -= align
    assert dim % t == 0, (dim, pref, align)
    return t


def _rms(x, g, eps=NORM_EPS):
    ms = jnp.mean(x * x, axis=-1, keepdims=True)
    return x * lax.rsqrt(ms + eps) * g


def _row(stack):
    return stack.reshape(stack.shape[0], 1, stack.shape[-1])


def _rope_table_kernel(pos_ref, invf_ref, cos_ref, sa_ref, sb_ref):
    ang = pos_ref[...].astype(F32) * invf_ref[...]
    lane = lax.broadcasted_iota(jnp.int32, ang.shape, 1)
    s = jnp.sin(ang)
    cos = jnp.cos(ang)
    sa = jnp.where(lane >= ROPE_HALF, s, 0.0)
    sb = jnp.where(lane < ROPE_HALF, -s, 0.0)
    for v, scale in enumerate(ROPE_SCALES):
        cos_ref[v], sa_ref[v], sb_ref[v] = cos * scale, sa * scale, sb * scale
    cos_ref[ROPE_NONE] = jnp.ones(cos.shape, F32)
    sa_ref[ROPE_NONE] = jnp.zeros(cos.shape, F32)
    sb_ref[ROPE_NONE] = jnp.zeros(cos.shape, F32)


def _rope_tables(positions):
    s = positions.shape[-1]
    inv_freq = ROPE_THETA ** (-jnp.arange(0, ROPE_DIM, 2, dtype=F32) / ROPE_DIM)
    invf = jnp.zeros((1, LANES), F32).at[0, :ROPE_DIM].set(jnp.concatenate([inv_freq, inv_freq]))
    nv = len(ROPE_SCALES) + 1
    tm = _tile(s, 1024, 8)
    tab = jax.ShapeDtypeStruct((nv, s, LANES), F32)
    return pl.pallas_call(
        _rope_table_kernel,
        grid=(s // tm,),
        in_specs=[pl.BlockSpec((tm, 1), lambda i: (i, 0)), pl.BlockSpec((1, LANES), lambda i: (0, 0))],
        out_specs=[pl.BlockSpec((nv, tm, LANES), lambda i: (0, i, 0))] * 3,
        out_shape=[tab] * 3,
        compiler_params=_params(("arbitrary",), (6 * nv + 8) * tm * LANES * 4),
        name="rope_tables",
    )(positions.reshape(s, 1), invf)


def _norm_to_scratch(x_ref, g_ref, h_ref):
    @pl.when(pl.program_id(1) == 0)
    def _():
        h_ref[...] = _rms(x_ref[...], g_ref[...]).astype(h_ref.dtype)


def _norm_matmul_kernel(x_ref, g_ref, w_ref, o_ref, h_ref):
    _norm_to_scratch(x_ref, g_ref, h_ref)
    o_ref[...] = jnp.dot(h_ref[...], w_ref[...], preferred_element_type=F32).astype(o_ref.dtype)


def _norm_matmul(x, g, w, li, *, tm_pref=512, tn_pref=1024):
    m, d = x.shape
    n = w.shape[-1]
    tm, tn = _tile(m, tm_pref, 8), _tile(n, tn_pref, LANES)
    vmem = 2 * tm * d * 4 + tm * d * 2 + 2 * d * tn * 2 + 2 * tm * tn * 2 + 2 * tm * tn * 4
    return pl.pallas_call(
        _norm_matmul_kernel,
        grid=(m // tm, n // tn),
        in_specs=[pl.BlockSpec((tm, d), lambda i, j: (i, 0)),
                  pl.BlockSpec((None, 1, d), lambda i, j: (li, 0, 0)),
                  pl.BlockSpec((None, d, tn), lambda i, j: (li, 0, j))],
        out_specs=pl.BlockSpec((tm, tn), lambda i, j: (i, j)),
        out_shape=jax.ShapeDtypeStruct((m, n), BF16),
        scratch_shapes=[pltpu.VMEM((tm, d), BF16)],
        compiler_params=_params(("arbitrary", "arbitrary"), vmem),
        name="norm_matmul",
    )(x, _row(g), w)


def _qkv_rope_kernel(x_ref, g_ref, w_ref, cos_ref, sa_ref, sb_ref, o_ref, h_ref):
    _norm_to_scratch(x_ref, g_ref, h_ref)
    y = jnp.dot(h_ref[...], w_ref[...], preferred_element_type=F32)
    cos, sa, sb = cos_ref[...], sa_ref[...], sb_ref[...]
    for c in range(y.shape[1] // HEAD_DIM):
        sl = slice(c * HEAD_DIM, (c + 1) * HEAD_DIM)
        t = y[:, sl]
        r = t * cos + pltpu.roll(t, ROPE_HALF, 1) * sa + pltpu.roll(t, HEAD_DIM - ROPE_HALF, 1) * sb
        o_ref[:, sl] = r.astype(o_ref.dtype)


def _qkv_rope(x, g, w, li, tables, *, n_q, n_k, q_variant, tm_pref=512):
    m, d = x.shape
    n = w.shape[-1]
    tn = _tile(math.gcd(n_q, n_k, n - n_q - n_k), 512, HEAD_DIM)
    tm = _tile(m, tm_pref, 8)
    nqb, nkb = n_q // tn, n_k // tn

    def tab_index(i, j):
        return (jnp.where(j < nqb, q_variant, jnp.where(j < nqb + nkb, ROPE_K, ROPE_NONE)), i, 0)

    tab_spec = pl.BlockSpec((None, tm, LANES), tab_index)
    vmem = 2 * tm * d * 4 + tm * d * 2 + 2 * d * tn * 2 + 2 * tm * tn * 2 + 3 * tm * tn * 4
    return pl.pallas_call(
        _qkv_rope_kernel,
        grid=(m // tm, n // tn),
        in_specs=[pl.BlockSpec((tm, d), lambda i, j: (i, 0)),
                  pl.BlockSpec((None, 1, d), lambda i, j: (li, 0, 0)),
                  pl.BlockSpec((None, d, tn), lambda i, j: (li, 0, j)),
                  tab_spec, tab_spec, tab_spec],
        out_specs=pl.BlockSpec((tm, tn), lambda i, j: (i, j)),
        out_shape=jax.ShapeDtypeStruct((m, n), BF16),
        scratch_shapes=[pltpu.VMEM((tm, d), BF16)],
        compiler_params=_params(("arbitrary", "arbitrary"), vmem),
        name="qkv_rope",
    )(x, _row(g), w, *tables)


def _matmul_residual_kernel(a_ref, w_ref, x_ref, o_ref):
    o_ref[...] = x_ref[...] + jnp.dot(a_ref[...], w_ref[...], preferred_element_type=F32)


def _matmul_residual(a, w, li, x, *, tm_pref, tn_pref):
    m, kd = a.shape
    n = w.shape[-1]
    tm, tn = _tile(m, tm_pref, 8), _tile(n, tn_pref, LANES)
    vmem = 2 * tm * kd * 2 + 2 * kd * tn * 2 + 5 * tm * tn * 4
    return pl.pallas_call(
        _matmul_residual_kernel,
        grid=(m // tm, n // tn),
        in_specs=[pl.BlockSpec((tm, kd), lambda i, j: (i, 0)),
                  pl.BlockSpec((None, kd, tn), lambda i, j: (li, 0, j)),
                  pl.BlockSpec((tm, tn), lambda i, j: (i, j))],
        out_specs=pl.BlockSpec((tm, tn), lambda i, j: (i, j)),
        out_shape=jax.ShapeDtypeStruct((m, n), F32),
        compiler_params=_params(("arbitrary", "arbitrary"), vmem),
        name="matmul_residual",
    )(a, w, x)


def _swa_kernel(sink_ref, q_ref, kc_ref, kp_ref, vc_ref, vp_ref, o_ref, *, group):
    i, h = pl.program_id(0), pl.program_id(1)
    blk = SWA_BLOCK
    nsub = q_ref.shape[0] // blk
    row = lax.broadcasted_iota(jnp.int32, (blk, 2 * blk), 0)
    col = lax.broadcasted_iota(jnp.int32, (blk, 2 * blk), 1)
    rel = col - blk
    band = (rel <= row) & (row - rel < blk)
    nt = (((1,), (1,)), ((), ()))
    for c in range(nsub):
        rows = slice(c * blk, (c + 1) * blk)
        if c == 0:
            kprev, vprev = kp_ref[...], vp_ref[...]
            mask = band & (rel >= jnp.where(i > 0, -blk, 0))
        else:
            prev = slice((c - 1) * blk, c * blk)
            kprev, vprev = kc_ref[prev, :], vc_ref[prev, :]
            mask = band
        kw = jnp.concatenate([kprev, kc_ref[rows, :]], axis=0)
        vw = jnp.concatenate([vprev, vc_ref[rows, :]], axis=0)
        qs = jnp.concatenate([q_ref[rows, g * HEAD_DIM:(g + 1) * HEAD_DIM] for g in range(group)], axis=0)
        s = lax.dot_general(qs, kw, nt, preferred_element_type=F32)
        probs = []
        for g in range(group):
            sg = jnp.where(mask, s[g * blk:(g + 1) * blk], -jnp.inf)
            sink = sink_ref[h * group + g]
            mx = jnp.maximum(jnp.max(sg, axis=-1, keepdims=True), sink)
            p = jnp.exp(sg - mx)
            den = jnp.sum(p, axis=-1, keepdims=True) + jnp.exp(sink - mx)
            probs.append((p / den).astype(BF16))
        o = jnp.dot(jnp.concatenate(probs, axis=0), vw, preferred_element_type=F32)
        for g in range(group):
            o_ref[rows, g * HEAD_DIM:(g + 1) * HEAD_DIM] = o[g * blk:(g + 1) * blk].astype(o_ref.dtype)


def _swa_attention(qkv, sinks, *, hq, hkv, tq_pref=512):
    s = qkv.shape[0]
    group = hq // hkv
    tq = _tile(s, tq_pref, SWA_BLOCK)
    r = tq // SWA_BLOCK
    cur = lambda off: pl.BlockSpec((tq, HEAD_DIM), lambda i, h, sk: (i, off + h))
    prev = lambda off: pl.BlockSpec((SWA_BLOCK, HEAD_DIM), lambda i, h, sk: (jnp.maximum(i * r - 1, 0), off + h))
    vmem = 4 * tq * group * HEAD_DIM * 2 + 8 * tq * HEAD_DIM * 2 + 8 * group * SWA_BLOCK * 2 * SWA_BLOCK * 4
    return pl.pallas_call(
        functools.partial(_swa_kernel, group=group),
        grid_spec=pltpu.PrefetchScalarGridSpec(
            num_scalar_prefetch=1,
            grid=(s // tq, hkv),
            in_specs=[pl.BlockSpec((tq, group * HEAD_DIM), lambda i, h, sk: (i, h)),
                      cur(hq), prev(hq), cur(hq + hkv), prev(hq + hkv)],
            out_specs=pl.BlockSpec((tq, group * HEAD_DIM), lambda i, h, sk: (i, h)),
        ),
        out_shape=jax.ShapeDtypeStruct((s, hq * HEAD_DIM), BF16),
        compiler_params=_params(("arbitrary", "arbitrary"), vmem),
        name="swa_attention",
    )(sinks, qkv, qkv, qkv, qkv, qkv)


def _diff_kernel(lam_ref, subln_ref, q_ref, k_ref, v_ref, o_ref, qs_ref, m_ref, l_ref, acc_ref,
                 *, group, tk, lambda_init):
    qi = pl.program_id(1)
    tq = q_ref.shape[0]
    rows = group * tq
    nt = (((1,), (1,)), ((), ()))

    for mp in range(2):
        for g in range(group):
            c = (g * 2 + mp) * HEAD_DIM
            qs_ref[mp, g * tq:(g + 1) * tq, :] = q_ref[:, c:c + HEAD_DIM]
    m_ref[...] = jnp.full(m_ref.shape, -jnp.inf, F32)
    l_ref[...] = jnp.zeros(l_ref.shape, F32)
    acc_ref[...] = jnp.zeros(acc_ref.shape, F32)

    def steps(kb0, count, masked):
        starts = [pl.multiple_of((kb0 + c) * tk, tk) for c in range(count)]
        scores = [[lax.dot_general(qs_ref[mp], k_ref[pl.ds(st, tk), mp * HEAD_DIM:(mp + 1) * HEAD_DIM], nt,
                                   preferred_element_type=F32) for mp in range(2)] for st in starts]
        for c, st in enumerate(starts):
            vblk = v_ref[pl.ds(st, tk), :]
            if masked:
                qpos = qi * tq + (lax.broadcasted_iota(jnp.int32, (rows, tk), 0) & (tq - 1))
                kpos = st + lax.broadcasted_iota(jnp.int32, (rows, tk), 1)
                visible = kpos <= qpos
            for mp in range(2):
                s = scores[c][mp]
                if masked:
                    s = jnp.where(visible, s, -jnp.inf)
                m_prev = m_ref[mp]
                m_next = jnp.maximum(m_prev, jnp.max(s, axis=-1, keepdims=True))
                alpha = jnp.exp2(m_prev - m_next)
                p = jnp.exp2(s - pltpu.repeat(m_next, tk // LANES, 1))
                lane_sum = p[:, :LANES]
                for t in range(1, tk // LANES):
                    lane_sum = lane_sum + p[:, t * LANES:(t + 1) * LANES]
                l_ref[mp] = alpha * l_ref[mp] + lane_sum
                pv = jnp.dot(p.astype(BF16), vblk, preferred_element_type=F32)
                acc_ref[mp] = acc_ref[mp] * pltpu.repeat(alpha, acc_ref.shape[2] // LANES, 1) + pv
                m_ref[mp] = m_next

    n_full = (qi * tq) // tk

    def body(pair, carry):
        steps(2 * pair, 2, False)
        return carry

    lax.fori_loop(0, n_full // 2, body, 0)

    @pl.when(n_full % 2 == 1)
    def _():
        steps(n_full - 1, 1, False)

    steps(n_full, 1, True)

    lam_vecs = lam_ref[...]
    lam = (jnp.exp(jnp.sum(lam_vecs[0:1] * lam_vecs[1:2], axis=-1, keepdims=True))
           - jnp.exp(jnp.sum(lam_vecs[2:3] * lam_vecs[3:4], axis=-1, keepdims=True)) + lambda_init)
    o = (acc_ref[0] / jnp.sum(l_ref[0], axis=-1, keepdims=True)
         - lam * (acc_ref[1] / jnp.sum(l_ref[1], axis=-1, keepdims=True)))
    y = _rms(o, subln_ref[...]) * (1.0 - lambda_init)
    vd = y.shape[1]
    for g in range(group):
        o_ref[:, g * vd:(g + 1) * vd] = y[g * tq:(g + 1) * tq].astype(o_ref.dtype)


def _diff_attention(qkv, lam_vecs, subln, li, *, heads, hkv, lambda_init, tq_pref=256, tk_pref=512):
    s = qkv.shape[0]
    group = heads // hkv
    vd = 2 * HEAD_DIM
    tq = _tile(s, tq_pref, 8)
    tk = _tile(s, tk_pref, LANES)
    assert tk % tq == 0 and tq & (tq - 1) == 0, (tq, tk)
    rows = group * tq
    vmem = (2 * s * vd * 2 + 4 * tq * group * vd * 2 + 2 * rows * HEAD_DIM * 2
            + 4 * rows * LANES * 4 + 2 * rows * vd * 4 + 6 * rows * tk * 4)
    kern = functools.partial(_diff_kernel, group=group, tk=tk, lambda_init=lambda_init)
    resident = lambda off: pl.BlockSpec((s, vd), lambda h, i: (0, off + h), pipeline_mode=pl.Buffered(1))
    return pl.pallas_call(
        kern,
        grid=(hkv, s // tq),
        in_specs=[pl.BlockSpec((4, HEAD_DIM), lambda h, i: (0, 0)),
                  pl.BlockSpec((None, 1, vd), lambda h, i: (li, 0, 0)),
                  pl.BlockSpec((tq, group * vd), lambda h, i: (i, h)),
                  resident(heads), resident(heads + hkv)],
        out_specs=pl.BlockSpec((tq, group * vd), lambda h, i: (i, h)),
        out_shape=jax.ShapeDtypeStruct((s, heads * vd), BF16),
        scratch_shapes=[pltpu.VMEM((2, rows, HEAD_DIM), BF16),
                        pltpu.VMEM((2, rows, LANES), F32),
                        pltpu.VMEM((2, rows, LANES), F32),
                        pltpu.VMEM((2, rows, vd), F32)],
        compiler_params=_params(("arbitrary", "arbitrary"), vmem),
        name="diff_attention",
    )(lam_vecs, _row(subln), qkv, qkv, qkv)


def _pool_kernel(x_ref, xp_ref, g_ref, w_ref, scale_ref, o_ref, ext_ref):
    i = pl.program_id(0)
    tm = x_ref.shape[0]
    ch = w_ref.shape[1]
    x = x_ref[...]
    h = _rms(x, g_ref[...])
    hp = _rms(xp_ref[...], g_ref[...])
    ext_ref[0:POOL_HALO, :] = jnp.where(i > 0, hp, 0.0)
    ext_ref[POOL_HALO:, :] = h
    t = i * tm + lax.broadcasted_iota(jnp.int32, (tm, 1), 0)
    for gi, win in enumerate(POOL_WINDOWS):
        cols = slice(gi * ch, (gi + 1) * ch)
        e = ext_ref[:, cols]
        span = 1
        while span < win:
            e = e + pltpu.roll(e, span, 0)
            span *= 2
        count = jnp.minimum(t + 1, win).astype(F32)
        pooled = (e[POOL_HALO:] / count - h[:, cols]).astype(BF16)
        y = jnp.dot(pooled, w_ref[gi], preferred_element_type=F32)
        o_ref[:, cols] = x[:, cols] + y * scale_ref[:, cols]


def _pool_mixer(x, g, w, scale, li, *, tm_pref=256):
    s, d = x.shape
    groups, ch = w.shape[1], w.shape[2]
    assert groups == len(POOL_WINDOWS)
    tm = _tile(s, tm_pref, POOL_HALO)
    r = tm // POOL_HALO
    vec = pl.BlockSpec((None, 1, d), lambda i: (li, 0, 0))
    vmem = 4 * tm * d * 4 + 2 * groups * ch * ch * 2 + (tm + POOL_HALO) * d * 4 + 6 * tm * ch * 4
    return pl.pallas_call(
        _pool_kernel,
        grid=(s // tm,),
        in_specs=[pl.BlockSpec((tm, d), lambda i: (i, 0)),
                  pl.BlockSpec((POOL_HALO, d), lambda i: (jnp.maximum(i * r - 1, 0), 0)),
                  vec,
                  pl.BlockSpec((None, groups, ch, ch), lambda i: (li, 0, 0, 0)),
                  vec],
        out_specs=pl.BlockSpec((tm, d), lambda i: (i, 0)),
        out_shape=jax.ShapeDtypeStruct((s, d), F32),
        scratch_shapes=[pltpu.VMEM((tm + POOL_HALO, d), F32)],
        compiler_params=_params(("arbitrary",), vmem),
        name="pool_mixer",
    )(x, x, _row(g), w, _row(scale))


def _xattn_kernel(x_ref, g_ref, wq_ref, kv_ref, wo_ref, gn_ref, o_ref, hn_ref, *, heads):
    x = x_ref[...]
    h = _rms(x, g_ref[...]).astype(BF16)
    width = wq_ref.shape[1]
    hd = width // heads
    q = (jnp.dot(h, wq_ref[...], preferred_element_type=F32) * (hd ** -0.5)).astype(BF16)
    nt = (((1,), (1,)), ((), ()))
    outs = []
    for a in range(heads):
        cols = slice(a * hd, (a + 1) * hd)
        s = lax.dot_general(q[:, cols], kv_ref[:, cols], nt, preferred_element_type=F32)
        p = jnp.exp(s - jnp.max(s, axis=-1, keepdims=True))
        den = jnp.sum(p, axis=-1, keepdims=True)
        pv = jnp.dot(p.astype(BF16), kv_ref[:, width + a * hd:width + (a + 1) * hd], preferred_element_type=F32)
        outs.append((pv / den).astype(BF16))
    o = jnp.concatenate(outs, axis=-1)
    y = x + jnp.dot(o, wo_ref[...], preferred_element_type=F32)
    o_ref[...] = y
    hn_ref[...] = _rms(y, gn_ref[...]).astype(hn_ref.dtype)


def _xattn(x, g, wq, kv, wo, g_next, li, *, tm_pref=256):
    s, d = x.shape
    width = wq.shape[-1]
    mem_len = kv.shape[0]
    tm = _tile(s, tm_pref, 8)
    vmem = (4 * tm * d * 4 + 2 * tm * d * 2 + 2 * d * width * 2 + mem_len * 2 * width * 2 + tm * d * 6
            + 4 * tm * width * 4)
    layer = lambda shape: pl.BlockSpec((None,) + shape, lambda i: (li, 0, 0), pipeline_mode=pl.Buffered(1))
    rows = pl.BlockSpec((tm, d), lambda i: (i, 0))
    return pl.pallas_call(
        functools.partial(_xattn_kernel, heads=X_HEADS),
        grid=(s // tm,),
        in_specs=[rows, layer((1, d)), layer((d, width)),
                  pl.BlockSpec((mem_len, 2 * width), lambda i: (0, 0), pipeline_mode=pl.Buffered(1)),
                  layer((width, d)), layer((1, d))],
        out_specs=[rows, rows],
        out_shape=[jax.ShapeDtypeStruct((s, d), F32), jax.ShapeDtypeStruct((s, d), BF16)],
        compiler_params=_params(("arbitrary",), vmem),
        name="xattn",
    )(x, _row(g), wq, kv, wo, _row(g_next))


def _ffn_in_kernel(h_ref, wg_ref, wv_ref, cwg_ref, cwv_ref, cbg_ref, cbv_ref, o_ref, cg_ref, cv_ref, *, chunk):
    i, j = pl.program_id(0), pl.program_id(1)
    tm = h_ref.shape[0]

    @pl.when(i == 0)
    def _():
        cg_ref[j] = jnp.zeros(cg_ref.shape[1:], F32)
        cv_ref[j] = jnp.zeros(cv_ref.shape[1:], F32)

    def conv(w_ref, cw_ref, cb_ref, carry_ref):
        u = jnp.dot(h_ref[...], w_ref[...], preferred_element_type=F32)
        first = carry_ref[j]
        carry_ref[j] = u[tm - CONV_HALO:, :]
        out = []
        for r0 in range(0, tm, chunk):
            prev = first if r0 == 0 else u[r0 - CONV_HALO:r0, :]
            ext = jnp.concatenate([prev, u[r0:r0 + chunk, :]], axis=0)
            c = cb_ref[...]
            for tap in range(CONV_WIDTH):
                lag = CONV_WIDTH - 1 - tap
                shifted = pltpu.roll(ext, lag, 0) if lag else ext
                c = c + cw_ref[tap:tap + 1, :] * shifted[CONV_HALO:, :]
            out.append(c)
        return out

    gate = conv(wg_ref, cwg_ref, cbg_ref, cg_ref)
    val = conv(wv_ref, cwv_ref, cbv_ref, cv_ref)
    for k, (gk, vk) in enumerate(zip(gate, val)):
        o_ref[k * chunk:(k + 1) * chunk, :] = (gk / (1.0 + jnp.exp(-gk)) * vk).astype(o_ref.dtype)


def _ffn_in(h, w_in, conv_w, conv_b, li, *, tm_pref=1024, tn_pref=512, chunk_pref=256):
    m, d = h.shape
    f = w_in.shape[-1] // 2
    tm, tn = _tile(m, tm_pref, 8), _tile(f, tn_pref, LANES)
    chunk = _tile(tm, chunk_pref, 8)
    nj = f // tn
    lo = lambda rows: pl.BlockSpec((None, rows, tn), lambda i, j: (li, 0, j))
    hi = lambda rows: pl.BlockSpec((None, rows, tn), lambda i, j: (li, 0, j + nj))
    vmem = 2 * tm * d * 2 + 4 * d * tn * 2 + 2 * tm * tn * 2 + 2 * nj * CONV_HALO * tn * 4 + 6 * tm * tn * 4
    return pl.pallas_call(
        functools.partial(_ffn_in_kernel, chunk=chunk),
        grid=(m // tm, nj),
        in_specs=[pl.BlockSpec((tm, d), lambda i, j: (i, 0)),
                  lo(d), hi(d), lo(CONV_WIDTH), hi(CONV_WIDTH), lo(1), hi(1)],
        out_specs=pl.BlockSpec((tm, tn), lambda i, j: (i, j)),
        out_shape=jax.ShapeDtypeStruct((m, f), BF16),
        scratch_shapes=[pltpu.VMEM((nj, CONV_HALO, tn), F32),
                        pltpu.VMEM((nj, CONV_HALO, tn), F32)],
        compiler_params=_params(("arbitrary", "arbitrary"), vmem),
        name="ffn_in",
    )(h, w_in, w_in, conv_w, conv_w, _row(conv_b), _row(conv_b))


def _final_norm_kernel(x_ref, g_ref, o_ref):
    o_ref[...] = _rms(x_ref[...], g_ref[...])


def _final_norm(x, g, *, tm_pref=512):
    s, d = x.shape
    tm = _tile(s, tm_pref, 8)
    return pl.pallas_call(
        _final_norm_kernel,
        grid=(s // tm,),
        in_specs=[pl.BlockSpec((tm, d), lambda i: (i, 0)), pl.BlockSpec((1, d), lambda i: (0, 0))],
        out_specs=pl.BlockSpec((tm, d), lambda i: (i, 0)),
        out_shape=jax.ShapeDtypeStruct((s, d), F32),
        compiler_params=_params(("arbitrary",), 6 * tm * d * 4),
        name="final_norm",
    )(x, g.reshape(1, d))


def kernel(x, mem, positions, swa_norm, swa_w_qkv, swa_sinks, swa_w_o, diff_norm, diff_w_qkv, diff_lambda_q1, diff_lambda_k1, diff_lambda_q2, diff_lambda_k2, diff_subln, diff_w_o, pool_norm, pool_w, pool_scale, xattn_norm, xattn_mem_norm, xattn_w_q, xattn_w_kv, xattn_w_o, ffn_norm, ffn_w_in, ffn_conv_w, ffn_conv_b, ffn_w_out, final_norm):
    batch, seq, d = x.shape
    assert batch == 1, "the kernels fold no batch dimension"
    depth = ffn_w_in.shape[0]
    xs = x.reshape(seq, d)
    mem2 = mem.reshape(mem.shape[1], d)
    tables = _rope_tables(positions)
    bf = lambda w: w.astype(BF16)
    swa_w_qkv, swa_w_o, diff_w_qkv, diff_w_o, pool_w = map(bf, (swa_w_qkv, swa_w_o, diff_w_qkv, diff_w_o, pool_w))
    xattn_w_q, xattn_w_kv, xattn_w_o, ffn_w_in, ffn_w_out = map(
        bf, (xattn_w_q, xattn_w_kv, xattn_w_o, ffn_w_in, ffn_w_out))

    swa_hq = d // HEAD_DIM
    swa_hkv = (swa_w_qkv.shape[-1] // HEAD_DIM - swa_hq) // 2
    diff_heads = d // (2 * HEAD_DIM)
    diff_hkv = (diff_w_qkv.shape[-1] // (2 * HEAD_DIM) - diff_heads) // 2

    for i in range(depth):
        kind, j = i % N_MIXERS, i // N_MIXERS
        if kind == 0:
            qkv = _qkv_rope(xs, swa_norm, swa_w_qkv, j, tables, n_q=swa_hq * HEAD_DIM, n_k=swa_hkv * HEAD_DIM,
                            q_variant=ROPE_Q_SWA)
            att = _swa_attention(qkv, swa_sinks[j], hq=swa_hq, hkv=swa_hkv)
            xs = _matmul_residual(att, swa_w_o, j, xs, tm_pref=1024, tn_pref=512)
        elif kind == 1:
            lambda_init = 0.8 - 0.6 * math.exp(-0.3 * i)
            qkv = _qkv_rope(xs, diff_norm, diff_w_qkv, j, tables, n_q=2 * diff_heads * HEAD_DIM,
                            n_k=2 * diff_hkv * HEAD_DIM, q_variant=ROPE_Q_DIFF)
            lam_vecs = jnp.stack([diff_lambda_q1[j], diff_lambda_k1[j], diff_lambda_q2[j], diff_lambda_k2[j]])
            att = _diff_attention(qkv, lam_vecs, diff_subln, j, heads=diff_heads, hkv=diff_hkv,
                                  lambda_init=lambda_init)
            xs = _matmul_residual(att, diff_w_o, j, xs, tm_pref=1024, tn_pref=512)
        else:
            xs = _pool_mixer(xs, pool_norm, pool_w, pool_scale, j)
        kv = _norm_matmul(mem2, xattn_mem_norm, xattn_w_kv, i)
        xs, hn = _xattn(xs, xattn_norm, xattn_w_q, kv, xattn_w_o, ffn_norm, i)
        act = _ffn_in(hn, ffn_w_in, ffn_conv_w, ffn_conv_b, i)
        xs = _matmul_residual(act, ffn_w_out, i, xs, tm_pref=512, tn_pref=256)
    return _final_norm(xs, final_norm).reshape(batch, seq, d)
```

```python
import functools
import math

import jax
import jax.numpy as jnp
from jax import lax
from jax.experimental import pallas as pl
from jax.experimental.pallas import tpu as pltpu

HEAD_DIM = 128
ROPE_DIM = HEAD_DIM // 4
ROPE_HALF = ROPE_DIM // 2
ROPE_THETA = 500000.0
SWA_BLOCK = 128
POOL_WINDOWS = (2, 4, 8, 16)
POOL_HALO = 16
X_HEADS = 4
CONV_WIDTH = 3
CONV_HALO = 8
N_MIXERS = 3
NORM_EPS = 1e-5
LANES = 128
LOG2E = math.log2(math.e)

V7X_VMEM_BYTES = 64 * 2**20
VMEM_RESERVE_BYTES = 6 * 2**20

F32 = jnp.float32
BF16 = jnp.bfloat16

ROPE_Q_SWA, ROPE_Q_DIFF, ROPE_K, ROPE_NONE = 0, 1, 2, 3
ROPE_SCALES = (HEAD_DIM ** -0.5, HEAD_DIM ** -0.5 * LOG2E, 1.0)


def _params(semantics, vmem_bytes, flags=None):
    limit = min(int(vmem_bytes) * 5 // 4 + VMEM_RESERVE_BYTES, V7X_VMEM_BYTES - VMEM_RESERVE_BYTES // 2)
    return pltpu.CompilerParams(dimension_semantics=semantics, vmem_limit_bytes=limit, flags=flags)


def _tile(dim, pref, align):
    if dim <= pref:
        return dim
    t = (pref // align) * align
    while t > align and dim % t:
        t -= align
    assert dim % t == 0, (dim, pref, align)
    return t


def _rms(x, g, eps=NORM_EPS):
    ms = jnp.mean(x * x, axis=-1, keepdims=True)
    return x * lax.rsqrt(ms + eps) * g


def _row(stack):
    return stack.reshape(stack.shape[0], 1, stack.shape[-1])


def _rope_table_kernel(pos_ref, invf_ref, cos_ref, sa_ref, sb_ref):
    ang = pos_ref[...].astype(F32) * invf_ref[...]
    lane = lax.broadcasted_iota(jnp.int32, ang.shape, 1)
    s = jnp.sin(ang)
    cos = jnp.cos(ang)
    sa = jnp.where(lane >= ROPE_HALF, s, 0.0)
    sb = jnp.where(lane < ROPE_HALF, -s, 0.0)
    for v, scale in enumerate(ROPE_SCALES):
        cos_ref[v], sa_ref[v], sb_ref[v] = cos * scale, sa * scale, sb * scale
    cos_ref[ROPE_NONE] = jnp.ones(cos.shape, F32)
    sa_ref[ROPE_NONE] = jnp.zeros(cos.shape, F32)
    sb_ref[ROPE_NONE] = jnp.zeros(cos.shape, F32)


def _rope_tables(positions):
    s = positions.shape[-1]
    inv_freq = ROPE_THETA ** (-jnp.arange(0, ROPE_DIM, 2, dtype=F32) / ROPE_DIM)
    invf = jnp.zeros((1, LANES), F32).at[0, :ROPE_DIM].set(jnp.concatenate([inv_freq, inv_freq]))
    nv = len(ROPE_SCALES) + 1
    tm = _tile(s, 1024, 8)
    tab = jax.ShapeDtypeStruct((nv, s, LANES), F32)
    return pl.pallas_call(
        _rope_table_kernel,
        grid=(s // tm,),
        in_specs=[pl.BlockSpec((tm, 1), lambda i: (i, 0)), pl.BlockSpec((1, LANES), lambda i: (0, 0))],
        out_specs=[pl.BlockSpec((nv, tm, LANES), lambda i: (0, i, 0))] * 3,
        out_shape=[tab] * 3,
        compiler_params=_params(("arbitrary",), (6 * nv + 8) * tm * LANES * 4),
        name="rope_tables",
    )(positions.reshape(s, 1), invf)


def _norm_to_scratch(x_ref, g_ref, h_ref):
    @pl.when(pl.program_id(1) == 0)
    def _():
        h_ref[...] = _rms(x_ref[...], g_ref[...]).astype(h_ref.dtype)


def _norm_matmul_kernel(x_ref, g_ref, w_ref, o_ref, h_ref):
    _norm_to_scratch(x_ref, g_ref, h_ref)
    o_ref[...] = jnp.dot(h_ref[...], w_ref[...], preferred_element_type=F32).astype(o_ref.dtype)


def _norm_matmul(x, g, w, li, *, tm_pref=512, tn_pref=1024):
    m, d = x.shape
    n = w.shape[-1]
    tm, tn = _tile(m, tm_pref, 8), _tile(n, tn_pref, LANES)
    vmem = 2 * tm * d * 4 + tm * d * 2 + 2 * d * tn * 2 + 2 * tm * tn * 2 + 2 * tm * tn * 4
    return pl.pallas_call(
        _norm_matmul_kernel,
        grid=(m // tm, n // tn),
        in_specs=[pl.BlockSpec((tm, d), lambda i, j: (i, 0)),
                  pl.BlockSpec((None, 1, d), lambda i, j: (li, 0, 0)),
                  pl.BlockSpec((None, d, tn), lambda i, j: (li, 0, j))],
        out_specs=pl.BlockSpec((tm, tn), lambda i, j: (i, j)),
        out_shape=jax.ShapeDtypeStruct((m, n), BF16),
        scratch_shapes=[pltpu.VMEM((tm, d), BF16)],
        compiler_params=_params(("arbitrary", "arbitrary"), vmem),
        name="norm_matmul",
    )(x, _row(g), w)


def _qkv_rope_kernel(x_ref, g_ref, w_ref, cos_ref, sa_ref, sb_ref, o_ref, h_ref):
    _norm_to_scratch(x_ref, g_ref, h_ref)
    y = jnp.dot(h_ref[...], w_ref[...], preferred_element_type=F32)
    cos, sa, sb = cos_ref[...], sa_ref[...], sb_ref[...]
    for c in range(y.shape[1] // HEAD_DIM):
        sl = slice(c * HEAD_DIM, (c + 1) * HEAD_DIM)
        t = y[:, sl]
        r = t * cos + pltpu.roll(t, ROPE_HALF, 1) * sa + pltpu.roll(t, HEAD_DIM - ROPE_HALF, 1) * sb
        o_ref[:, sl] = r.astype(o_ref.dtype)


def _qkv_rope(x, g, w, li, tables, *, n_q, n_k, q_variant, tm_pref=512):
    m, d = x.shape
    n = w.shape[-1]
    tn = _tile(math.gcd(n_q, n_k, n - n_q - n_k), 512, HEAD_DIM)
    tm = _tile(m, tm_pref, 8)
    nqb, nkb = n_q // tn, n_k // tn

    def tab_index(i, j):
        return (jnp.where(j < nqb, q_variant, jnp.where(j < nqb + nkb, ROPE_K, ROPE_NONE)), i, 0)

    tab_spec = pl.BlockSpec((None, tm, LANES), tab_index)
    vmem = 2 * tm * d * 4 + tm * d * 2 + 2 * d * tn * 2 + 2 * tm * tn * 2 + 3 * tm * tn * 4
    return pl.pallas_call(
        _qkv_rope_kernel,
        grid=(m // tm, n // tn),
        in_specs=[pl.BlockSpec((tm, d), lambda i, j: (i, 0)),
                  pl.BlockSpec((None, 1, d), lambda i, j: (li, 0, 0)),
                  pl.BlockSpec((None, d, tn), lambda i, j: (li, 0, j)),
                  tab_spec, tab_spec, tab_spec],
        out_specs=pl.BlockSpec((tm, tn), lambda i, j: (i, j)),
        out_shape=jax.ShapeDtypeStruct((m, n), BF16),
        scratch_shapes=[pltpu.VMEM((tm, d), BF16)],
        compiler_params=_params(("arbitrary", "arbitrary"), vmem),
        name="qkv_rope",
    )(x, _row(g), w, *tables)


def _matmul_residual_kernel(a_ref, w_ref, x_ref, o_ref):
    o_ref[...] = x_ref[...] + jnp.dot(a_ref[...], w_ref[...], preferred_element_type=F32)


def _matmul_residual(a, w, li, x, *, tm_pref, tn_pref):
    m, kd = a.shape
    n = w.shape[-1]
    tm, tn = _tile(m, tm_pref, 8), _tile(n, tn_pref, LANES)
    vmem = 2 * tm * kd * 2 + 2 * kd * tn * 2 + 5 * tm * tn * 4
    return pl.pallas_call(
        _matmul_residual_kernel,
        grid=(m // tm, n // tn),
        in_specs=[pl.BlockSpec((tm, kd), lambda i, j: (i, 0)),
                  pl.BlockSpec((None, kd, tn), lambda i, j: (li, 0, j)),
                  pl.BlockSpec((tm, tn), lambda i, j: (i, j))],
        out_specs=pl.BlockSpec((tm, tn), lambda i, j: (i, j)),
        out_shape=jax.ShapeDtypeStruct((m, n), F32),
        compiler_params=_params(("arbitrary", "arbitrary"), vmem),
        name="matmul_residual",
    )(a, w, x)


def _swa_kernel(sink_ref, q_ref, kc_ref, kp_ref, vc_ref, vp_ref, o_ref, *, group):
    i, h = pl.program_id(0), pl.program_id(1)
    blk = SWA_BLOCK
    nsub = q_ref.shape[0] // blk
    row = lax.broadcasted_iota(jnp.int32, (blk, 2 * blk), 0)
    col = lax.broadcasted_iota(jnp.int32, (blk, 2 * blk), 1)
    rel = col - blk
    band = (rel <= row) & (row - rel < blk)
    nt = (((1,), (1,)), ((), ()))
    for c in range(nsub):
        rows = slice(c * blk, (c + 1) * blk)
        if c == 0:
            kprev, vprev = kp_ref[...], vp_ref[...]
            mask = band & (rel >= jnp.where(i > 0, -blk, 0))
        else:
            prev = slice((c - 1) * blk, c * blk)
            kprev, vprev = kc_ref[prev, :], vc_ref[prev, :]
            mask = band
        kw = jnp.concatenate([kprev, kc_ref[rows, :]], axis=0)
        vw = jnp.concatenate([vprev, vc_ref[rows, :]], axis=0)
        qs = jnp.concatenate([q_ref[rows, g * HEAD_DIM:(g + 1) * HEAD_DIM] for g in range(group)], axis=0)
        s = lax.dot_general(qs, kw, nt, preferred_element_type=F32)
        probs = []
        for g in range(group):
            sg = jnp.where(mask, s[g * blk:(g + 1) * blk], -jnp.inf)
            sink = sink_ref[h * group + g]
            mx = jnp.maximum(jnp.max(sg, axis=-1, keepdims=True), sink)
            p = jnp.exp(sg - mx)
            den = jnp.sum(p, axis=-1, keepdims=True) + jnp.exp(sink - mx)
            probs.append((p / den).astype(BF16))
        o = jnp.dot(jnp.concatenate(probs, axis=0), vw, preferred_element_type=F32)
        for g in range(group):
            o_ref[rows, g * HEAD_DIM:(g + 1) * HEAD_DIM] = o[g * blk:(g + 1) * blk].astype(o_ref.dtype)


def _swa_attention(qkv, sinks, *, hq, hkv, tq_pref=512):
    s = qkv.shape[0]
    group = hq // hkv
    tq = _tile(s, tq_pref, SWA_BLOCK)
    r = tq // SWA_BLOCK
    cur = lambda off: pl.BlockSpec((tq, HEAD_DIM), lambda i, h, sk: (i, off + h))
    prev = lambda off: pl.BlockSpec((SWA_BLOCK, HEAD_DIM), lambda i, h, sk: (jnp.maximum(i * r - 1, 0), off + h))
    vmem = 4 * tq * group * HEAD_DIM * 2 + 8 * tq * HEAD_DIM * 2 + 8 * group * SWA_BLOCK * 2 * SWA_BLOCK * 4
    return pl.pallas_call(
        functools.partial(_swa_kernel, group=group),
        grid_spec=pltpu.PrefetchScalarGridSpec(
            num_scalar_prefetch=1,
            grid=(s // tq, hkv),
            in_specs=[pl.BlockSpec((tq, group * HEAD_DIM), lambda i, h, sk: (i, h)),
                      cur(hq), prev(hq), cur(hq + hkv), prev(hq + hkv)],
            out_specs=pl.BlockSpec((tq, group * HEAD_DIM), lambda i, h, sk: (i, h)),
        ),
        out_shape=jax.ShapeDtypeStruct((s, hq * HEAD_DIM), BF16),
        compiler_params=_params(("arbitrary", "arbitrary"), vmem),
        name="swa_attention",
    )(sinks, qkv, qkv, qkv, qkv, qkv)


def _diff_kernel(lam_ref, subln_ref, q_ref, k_ref, v_ref, o_ref, qs_ref, m_ref, l_ref, acc_ref, sa_ref, sb_ref,
                 *, group, tk, lambda_init):
    qi = pl.program_id(1)
    tq = q_ref.shape[0]
    rows = group * tq
    nt = (((1,), (1,)), ((), ()))

    for mp in range(2):
        for g in range(group):
            c = (g * 2 + mp) * HEAD_DIM
            qs_ref[mp, g * tq:(g + 1) * tq, :] = q_ref[:, c:c + HEAD_DIM]
    m_ref[...] = jnp.full(m_ref.shape, -jnp.inf, F32)
    l_ref[...] = jnp.zeros(l_ref.shape, F32)
    acc_ref[...] = jnp.zeros(acc_ref.shape, F32)

    def scores(kb, dst_ref):
        st = pl.multiple_of(kb * tk, tk)
        for mp in range(2):
            dst_ref[mp] = lax.dot_general(qs_ref[mp], k_ref[pl.ds(st, tk), mp * HEAD_DIM:(mp + 1) * HEAD_DIM], nt,
                                          preferred_element_type=F32)

    def update(kb, src_ref, masked):
        st = pl.multiple_of(kb * tk, tk)
        vblk = v_ref[pl.ds(st, tk), :]
        if masked:
            qpos = qi * tq + (lax.broadcasted_iota(jnp.int32, (rows, tk), 0) & (tq - 1))
            kpos = st + lax.broadcasted_iota(jnp.int32, (rows, tk), 1)
            visible = kpos <= qpos
        for mp in range(2):
            s = src_ref[mp]
            if masked:
                s = jnp.where(visible, s, -jnp.inf)
            m_prev = m_ref[mp]
            m_next = jnp.maximum(m_prev, jnp.max(s, axis=-1, keepdims=True))
            alpha = jnp.exp2(m_prev - m_next)
            p = jnp.exp2(s - pltpu.repeat(m_next, tk // LANES, 1))
            lane_sum = p[:, :LANES]
            for t in range(1, tk // LANES):
                lane_sum = lane_sum + p[:, t * LANES:(t + 1) * LANES]
            l_ref[mp] = alpha * l_ref[mp] + lane_sum
            pv = jnp.dot(p.astype(BF16), vblk, preferred_element_type=F32)
            acc_ref[mp] = acc_ref[mp] * pltpu.repeat(alpha, acc_ref.shape[2] // LANES, 1) + pv
            m_ref[mp] = m_next

    n_full = (qi * tq) // tk
    scores(0, sa_ref)

    def body(t, carry):
        @pl.when(t % 2 == 0)
        def _():
            scores(t + 1, sb_ref)
            update(t, sa_ref, False)

        @pl.when(t % 2 == 1)
        def _():
            scores(t + 1, sa_ref)
            update(t, sb_ref, False)

        return carry

    lax.fori_loop(0, n_full, body, 0)

    @pl.when(n_full % 2 == 0)
    def _():
        update(n_full, sa_ref, True)

    @pl.when(n_full % 2 == 1)
    def _():
        update(n_full, sb_ref, True)

    lam_vecs = lam_ref[...]
    lam = (jnp.exp(jnp.sum(lam_vecs[0:1] * lam_vecs[1:2], axis=-1, keepdims=True))
           - jnp.exp(jnp.sum(lam_vecs[2:3] * lam_vecs[3:4], axis=-1, keepdims=True)) + lambda_init)
    o = (acc_ref[0] / jnp.sum(l_ref[0], axis=-1, keepdims=True)
         - lam * (acc_ref[1] / jnp.sum(l_ref[1], axis=-1, keepdims=True)))
    y = _rms(o, subln_ref[...]) * (1.0 - lambda_init)
    vd = y.shape[1]
    for g in range(group):
        o_ref[:, g * vd:(g + 1) * vd] = y[g * tq:(g + 1) * tq].astype(o_ref.dtype)


def _diff_attention(qkv, lam_vecs, subln, li, *, heads, hkv, lambda_init, tq_pref=256, tk_pref=1024):
    s = qkv.shape[0]
    group = heads // hkv
    vd = 2 * HEAD_DIM
    tq = _tile(s, tq_pref, 8)
    tk = _tile(s, tk_pref, LANES)
    assert tk % tq == 0 and tq & (tq - 1) == 0, (tq, tk)
    rows = group * tq
    vmem = (2 * s * vd * 2 + 4 * tq * group * vd * 2 + 2 * rows * HEAD_DIM * 2
            + 4 * rows * LANES * 4 + 2 * rows * vd * 4 + 4 * rows * tk * 4 + 2 * rows * tk * 4)
    kern = functools.partial(_diff_kernel, group=group, tk=tk, lambda_init=lambda_init)
    resident = lambda off: pl.BlockSpec((s, vd), lambda h, i: (0, off + h), pipeline_mode=pl.Buffered(1))
    return pl.pallas_call(
        kern,
        grid=(hkv, s // tq),
        in_specs=[pl.BlockSpec((4, HEAD_DIM), lambda h, i: (0, 0)),
                  pl.BlockSpec((None, 1, vd), lambda h, i: (li, 0, 0)),
                  pl.BlockSpec((tq, group * vd), lambda h, i: (i, h)),
                  resident(heads), resident(heads + hkv)],
        out_specs=pl.BlockSpec((tq, group * vd), lambda h, i: (i, h)),
        out_shape=jax.ShapeDtypeStruct((s, heads * vd), BF16),
        scratch_shapes=[pltpu.VMEM((2, rows, HEAD_DIM), BF16),
                        pltpu.VMEM((2, rows, LANES), F32),
                        pltpu.VMEM((2, rows, LANES), F32),
                        pltpu.VMEM((2, rows, vd), F32),
                        pltpu.VMEM((2, rows, tk), F32),
                        pltpu.VMEM((2, rows, tk), F32)],
        compiler_params=_params(("arbitrary", "arbitrary"), vmem),
        name="diff_attention",
    )(lam_vecs, _row(subln), qkv, qkv, qkv)


def _pool_kernel(x_ref, xp_ref, g_ref, w_ref, scale_ref, o_ref, ext_ref):
    i = pl.program_id(0)
    tm = x_ref.shape[0]
    ch = w_ref.shape[1]
    x = x_ref[...]
    h = _rms(x, g_ref[...])
    hp = _rms(xp_ref[...], g_ref[...])
    ext_ref[0:POOL_HALO, :] = jnp.where(i > 0, hp, 0.0)
    ext_ref[POOL_HALO:, :] = h
    t = i * tm + lax.broadcasted_iota(jnp.int32, (tm, 1), 0)
    for gi, win in enumerate(POOL_WINDOWS):
        cols = slice(gi * ch, (gi + 1) * ch)
        e = ext_ref[:, cols]
        span = 1
        while span < win:
            e = e + pltpu.roll(e, span, 0)
            span *= 2
        count = jnp.minimum(t + 1, win).astype(F32)
        pooled = (e[POOL_HALO:] / count - h[:, cols]).astype(BF16)
        y = jnp.dot(pooled, w_ref[gi], preferred_element_type=F32)
        o_ref[:, cols] = x[:, cols] + y * scale_ref[:, cols]


def _pool_mixer(x, g, w, scale, li, *, tm_pref=256):
    s, d = x.shape
    groups, ch = w.shape[1], w.shape[2]
    assert groups == len(POOL_WINDOWS)
    tm = _tile(s, tm_pref, POOL_HALO)
    r = tm // POOL_HALO
    vec = pl.BlockSpec((None, 1, d), lambda i: (li, 0, 0))
    vmem = 4 * tm * d * 4 + 2 * groups * ch * ch * 2 + (tm + POOL_HALO) * d * 4 + 6 * tm * ch * 4
    return pl.pallas_call(
        _pool_kernel,
        grid=(s // tm,),
        in_specs=[pl.BlockSpec((tm, d), lambda i: (i, 0)),
                  pl.BlockSpec((POOL_HALO, d), lambda i: (jnp.maximum(i * r - 1, 0), 0)),
                  vec,
                  pl.BlockSpec((None, groups, ch, ch), lambda i: (li, 0, 0, 0)),
                  vec],
        out_specs=pl.BlockSpec((tm, d), lambda i: (i, 0)),
        out_shape=jax.ShapeDtypeStruct((s, d), F32),
        scratch_shapes=[pltpu.VMEM((tm + POOL_HALO, d), F32)],
        compiler_params=_params(("arbitrary",), vmem),
        name="pool_mixer",
    )(x, x, _row(g), w, _row(scale))


def _xattn_kernel(x_ref, g_ref, wq_ref, kv_ref, wo_ref, gn_ref, o_ref, hn_ref, *, heads):
    x = x_ref[...]
    h = _rms(x, g_ref[...]).astype(BF16)
    width = wq_ref.shape[1]
    hd = width // heads
    q = (jnp.dot(h, wq_ref[...], preferred_element_type=F32) * (hd ** -0.5)).astype(BF16)
    nt = (((1,), (1,)), ((), ()))
    outs = []
    for a in range(heads):
        cols = slice(a * hd, (a + 1) * hd)
        s = lax.dot_general(q[:, cols], kv_ref[:, cols], nt, preferred_element_type=F32)
        p = jnp.exp(s - jnp.max(s, axis=-1, keepdims=True))
        den = jnp.sum(p, axis=-1, keepdims=True)
        pv = jnp.dot(p.astype(BF16), kv_ref[:, width + a * hd:width + (a + 1) * hd], preferred_element_type=F32)
        outs.append((pv / den).astype(BF16))
    o = jnp.concatenate(outs, axis=-1)
    y = x + jnp.dot(o, wo_ref[...], preferred_element_type=F32)
    o_ref[...] = y
    hn_ref[...] = _rms(y, gn_ref[...]).astype(hn_ref.dtype)


def _xattn(x, g, wq, kv, wo, g_next, li, *, tm_pref=256):
    s, d = x.shape
    width = wq.shape[-1]
    mem_len = kv.shape[0]
    tm = _tile(s, tm_pref, 8)
    vmem = (4 * tm * d * 4 + 2 * tm * d * 2 + 2 * d * width * 2 + mem_len * 2 * width * 2 + tm * d * 6
            + 4 * tm * width * 4)
    layer = lambda shape: pl.BlockSpec((None,) + shape, lambda i: (li, 0, 0), pipeline_mode=pl.Buffered(1))
    rows = pl.BlockSpec((tm, d), lambda i: (i, 0))
    return pl.pallas_call(
        functools.partial(_xattn_kernel, heads=X_HEADS),
        grid=(s // tm,),
        in_specs=[rows, layer((1, d)), layer((d, width)),
                  pl.BlockSpec((mem_len, 2 * width), lambda i: (0, 0), pipeline_mode=pl.Buffered(1)),
                  layer((width, d)), layer((1, d))],
        out_specs=[rows, rows],
        out_shape=[jax.ShapeDtypeStruct((s, d), F32), jax.ShapeDtypeStruct((s, d), BF16)],
        compiler_params=_params(("arbitrary",), vmem),
        name="xattn",
    )(x, _row(g), wq, kv, wo, _row(g_next))


def _ffn_in_kernel(h_ref, wg_ref, wv_ref, cwg_ref, cwv_ref, cbg_ref, cbv_ref, o_ref, cg_ref, cv_ref, *, chunk):
    i, j = pl.program_id(0), pl.program_id(1)
    tm = h_ref.shape[0]

    @pl.when(i == 0)
    def _():
        cg_ref[j] = jnp.zeros(cg_ref.shape[1:], F32)
        cv_ref[j] = jnp.zeros(cv_ref.shape[1:], F32)

    def conv(w_ref, cw_ref, cb_ref, carry_ref):
        u = jnp.dot(h_ref[...], w_ref[...], preferred_element_type=F32)
        first = carry_ref[j]
        carry_ref[j] = u[tm - CONV_HALO:, :]
        out = []
        for r0 in range(0, tm, chunk):
            prev = first if r0 == 0 else u[r0 - CONV_HALO:r0, :]
            ext = jnp.concatenate([prev, u[r0:r0 + chunk, :]], axis=0)
            c = cb_ref[...]
            for tap in range(CONV_WIDTH):
                lag = CONV_WIDTH - 1 - tap
                shifted = pltpu.roll(ext, lag, 0) if lag else ext
                c = c + cw_ref[tap:tap + 1, :] * shifted[CONV_HALO:, :]
            out.append(c)
        return out

    gate = conv(wg_ref, cwg_ref, cbg_ref, cg_ref)
    val = conv(wv_ref, cwv_ref, cbv_ref, cv_ref)
    for k, (gk, vk) in enumerate(zip(gate, val)):
        o_ref[k * chunk:(k + 1) * chunk, :] = (gk / (1.0 + jnp.exp(-gk)) * vk).astype(o_ref.dtype)


def _ffn_in(h, w_in, conv_w, conv_b, li, *, tm_pref=1024, tn_pref=512, chunk_pref=256):
    m, d = h.shape
    f = w_in.shape[-1] // 2
    tm, tn = _tile(m, tm_pref, 8), _tile(f, tn_pref, LANES)
    chunk = _tile(tm, chunk_pref, 8)
    nj = f // tn
    lo = lambda rows: pl.BlockSpec((None, rows, tn), lambda i, j: (li, 0, j))
    hi = lambda rows: pl.BlockSpec((None, rows, tn), lambda i, j: (li, 0, j + nj))
    vmem = 2 * tm * d * 2 + 4 * d * tn * 2 + 2 * tm * tn * 2 + 2 * nj * CONV_HALO * tn * 4 + 6 * tm * tn * 4
    return pl.pallas_call(
        functools.partial(_ffn_in_kernel, chunk=chunk),
        grid=(m // tm, nj),
        in_specs=[pl.BlockSpec((tm, d), lambda i, j: (i, 0)),
                  lo(d), hi(d), lo(CONV_WIDTH), hi(CONV_WIDTH), lo(1), hi(1)],
        out_specs=pl.BlockSpec((tm, tn), lambda i, j: (i, j)),
        out_shape=jax.ShapeDtypeStruct((m, f), BF16),
        scratch_shapes=[pltpu.VMEM((nj, CONV_HALO, tn), F32),
                        pltpu.VMEM((nj, CONV_HALO, tn), F32)],
        compiler_params=_params(("arbitrary", "arbitrary"), vmem),
        name="ffn_in",
    )(h, w_in, w_in, conv_w, conv_w, _row(conv_b), _row(conv_b))


def _final_norm_kernel(x_ref, g_ref, o_ref):
    o_ref[...] = _rms(x_ref[...], g_ref[...])


def _final_norm(x, g, *, tm_pref=512):
    s, d = x.shape
    tm = _tile(s, tm_pref, 8)
    return pl.pallas_call(
        _final_norm_kernel,
        grid=(s // tm,),
        in_specs=[pl.BlockSpec((tm, d), lambda i: (i, 0)), pl.BlockSpec((1, d), lambda i: (0, 0))],
        out_specs=pl.BlockSpec((tm, d), lambda i: (i, 0)),
        out_shape=jax.ShapeDtypeStruct((s, d), F32),
        compiler_params=_params(("arbitrary",), 6 * tm * d * 4),
        name="final_norm",
    )(x, g.reshape(1, d))


def kernel(x, mem, positions, swa_norm, swa_w_qkv, swa_sinks, swa_w_o, diff_norm, diff_w_qkv, diff_lambda_q1, diff_lambda_k1, diff_lambda_q2, diff_lambda_k2, diff_subln, diff_w_o, pool_norm, pool_w, pool_scale, xattn_norm, xattn_mem_norm, xattn_w_q, xattn_w_kv, xattn_w_o, ffn_norm, ffn_w_in, ffn_conv_w, ffn_conv_b, ffn_w_out, final_norm):
    batch, seq, d = x.shape
    assert batch == 1, "the kernels fold no batch dimension"
    depth = ffn_w_in.shape[0]
    xs = x.reshape(seq, d)
    mem2 = mem.reshape(mem.shape[1], d)
    tables = _rope_tables(positions)
    bf = lambda w: w.astype(BF16)
    swa_w_qkv, swa_w_o, diff_w_qkv, diff_w_o, pool_w = map(bf, (swa_w_qkv, swa_w_o, diff_w_qkv, diff_w_o, pool_w))
    xattn_w_q, xattn_w_kv, xattn_w_o, ffn_w_in, ffn_w_out = map(
        bf, (xattn_w_q, xattn_w_kv, xattn_w_o, ffn_w_in, ffn_w_out))

    swa_hq = d // HEAD_DIM
    swa_hkv = (swa_w_qkv.shape[-1] // HEAD_DIM - swa_hq) // 2
    diff_heads = d // (2 * HEAD_DIM)
    diff_hkv = (diff_w_qkv.shape[-1] // (2 * HEAD_DIM) - diff_heads) // 2

    for i in range(depth):
        kind, j = i % N_MIXERS, i // N_MIXERS
        if kind == 0:
            qkv = _qkv_rope(xs, swa_norm, swa_w_qkv, j, tables, n_q=swa_hq * HEAD_DIM, n_k=swa_hkv * HEAD_DIM,
                            q_variant=ROPE_Q_SWA)
            att = _swa_attention(qkv, swa_sinks[j], hq=swa_hq, hkv=swa_hkv)
            xs = _matmul_residual(att, swa_w_o, j, xs, tm_pref=1024, tn_pref=512)
        elif kind == 1:
            lambda_init = 0.8 - 0.6 * math.exp(-0.3 * i)
            qkv = _qkv_rope(xs, diff_norm, diff_w_qkv, j, tables, n_q=2 * diff_heads * HEAD_DIM,
                            n_k=2 * diff_hkv * HEAD_DIM, q_variant=ROPE_Q_DIFF)
            lam_vecs = jnp.stack([diff_lambda_q1[j], diff_lambda_k1[j], diff_lambda_q2[j], diff_lambda_k2[j]])
            att = _diff_attention(qkv, lam_vecs, diff_subln, j, heads=diff_heads, hkv=diff_hkv,
                                  lambda_init=lambda_init)
            xs = _matmul_residual(att, diff_w_o, j, xs, tm_pref=1024, tn_pref=512)
        else:
            xs = _pool_mixer(xs, pool_norm, pool_w, pool_scale, j)
        kv = _norm_matmul(mem2, xattn_mem_norm, xattn_w_kv, i)
        xs, hn = _xattn(xs, xattn_norm, xattn_w_q, kv, xattn_w_o, ffn_norm, i)
        act = _ffn_in(hn, ffn_w_in, ffn_conv_w, ffn_conv_b, i)
        xs = _matmul_residual(act, ffn_w_out, i, xs, tm_pref=512, tn_pref=256)
    return _final_norm(xs, final_norm).reshape(batch, seq, d)
```

```python
import functools
import math

import jax
import jax.numpy as jnp
from jax import lax
from jax.experimental import pallas as pl
from jax.experimental.pallas import tpu as pltpu

HEAD_DIM = 128
ROPE_DIM = HEAD_DIM // 4
ROPE_HALF = ROPE_DIM // 2
ROPE_THETA = 500000.0
SWA_BLOCK = 128
POOL_WINDOWS = (2, 4, 8, 16)
POOL_HALO = 16
X_HEADS = 4
CONV_WIDTH = 3
CONV_HALO = 8
N_MIXERS = 3
NORM_EPS = 1e-5
LANES = 128
LOG2E = math.log2(math.e)

V7X_VMEM_BYTES = 64 * 2**20
VMEM_RESERVE_BYTES = 6 * 2**20

F32 = jnp.float32
BF16 = jnp.bfloat16

ROPE_Q_SWA, ROPE_Q_DIFF, ROPE_K, ROPE_NONE = 0, 1, 2, 3
ROPE_SCALES = (HEAD_DIM ** -0.5, HEAD_DIM ** -0.5 * LOG2E, 1.0)


def _params(semantics, vmem_bytes, flags=None):
    limit = min(int(vmem_bytes) * 5 // 4 + VMEM_RESERVE_BYTES, V7X_VMEM_BYTES - VMEM_RESERVE_BYTES // 2)
    return pltpu.CompilerParams(dimension_semantics=semantics, vmem_limit_bytes=limit, flags=flags)


def _tile(dim, pref, align):
    if dim <= pref:
        return dim
    t = (pref // align) * align
    while t > align and dim % t:
        t -= align
    assert dim % t == 0, (dim, pref, align)
    return t


def _rms(x, g, eps=NORM_EPS):
    ms = jnp.mean(x * x, axis=-1, keepdims=True)
    return x * lax.rsqrt(ms + eps) * g


def _row(stack):
    return stack.reshape(stack.shape[0], 1, stack.shape[-1])


def _rope_table_kernel(pos_ref, invf_ref, cos_ref, sa_ref, sb_ref):
    ang = pos_ref[...].astype(F32) * invf_ref[...]
    lane = lax.broadcasted_iota(jnp.int32, ang.shape, 1)
    s = jnp.sin(ang)
    cos = jnp.cos(ang)
    sa = jnp.where(lane >= ROPE_HALF, s, 0.0)
    sb = jnp.where(lane < ROPE_HALF, -s, 0.0)
    for v, scale in enumerate(ROPE_SCALES):
        cos_ref[v], sa_ref[v], sb_ref[v] = cos * scale, sa * scale, sb * scale
    cos_ref[ROPE_NONE] = jnp.ones(cos.shape, F32)
    sa_ref[ROPE_NONE] = jnp.zeros(cos.shape, F32)
    sb_ref[ROPE_NONE] = jnp.zeros(cos.shape, F32)


def _rope_tables(positions):
    s = positions.shape[-1]
    inv_freq = ROPE_THETA ** (-jnp.arange(0, ROPE_DIM, 2, dtype=F32) / ROPE_DIM)
    invf = jnp.zeros((1, LANES), F32).at[0, :ROPE_DIM].set(jnp.concatenate([inv_freq, inv_freq]))
    nv = len(ROPE_SCALES) + 1
    tm = _tile(s, 1024, 8)
    tab = jax.ShapeDtypeStruct((nv, s, LANES), F32)
    return pl.pallas_call(
        _rope_table_kernel,
        grid=(s // tm,),
        in_specs=[pl.BlockSpec((tm, 1), lambda i: (i, 0)), pl.BlockSpec((1, LANES), lambda i: (0, 0))],
        out_specs=[pl.BlockSpec((nv, tm, LANES), lambda i: (0, i, 0))] * 3,
        out_shape=[tab] * 3,
        compiler_params=_params(("arbitrary",), (6 * nv + 8) * tm * LANES * 4),
        name="rope_tables",
    )(positions.reshape(s, 1), invf)


def _norm_to_scratch(x_ref, g_ref, h_ref):
    @pl.when(pl.program_id(1) == 0)
    def _():
        h_ref[...] = _rms(x_ref[...], g_ref[...]).astype(h_ref.dtype)


def _norm_matmul_kernel(x_ref, g_ref, w_ref, o_ref, h_ref):
    _norm_to_scratch(x_ref, g_ref, h_ref)
    o_ref[...] = jnp.dot(h_ref[...], w_ref[...], preferred_element_type=F32).astype(o_ref.dtype)


def _norm_matmul(x, g, w, li, *, tm_pref=512, tn_pref=1024):
    m, d = x.shape
    n = w.shape[-1]
    tm, tn = _tile(m, tm_pref, 8), _tile(n, tn_pref, LANES)
    vmem = 2 * tm * d * 4 + tm * d * 2 + 2 * d * tn * 2 + 2 * tm * tn * 2 + 2 * tm * tn * 4
    return pl.pallas_call(
        _norm_matmul_kernel,
        grid=(m // tm, n // tn),
        in_specs=[pl.BlockSpec((tm, d), lambda i, j: (i, 0)),
                  pl.BlockSpec((None, 1, d), lambda i, j: (li, 0, 0)),
                  pl.BlockSpec((None, d, tn), lambda i, j: (li, 0, j))],
        out_specs=pl.BlockSpec((tm, tn), lambda i, j: (i, j)),
        out_shape=jax.ShapeDtypeStruct((m, n), BF16),
        scratch_shapes=[pltpu.VMEM((tm, d), BF16)],
        compiler_params=_params(("arbitrary", "arbitrary"), vmem),
        name="norm_matmul",
    )(x, _row(g), w)


def _qkv_rope_kernel(x_ref, g_ref, w_ref, cos_ref, sa_ref, sb_ref, o_ref, h_ref):
    _norm_to_scratch(x_ref, g_ref, h_ref)
    y = jnp.dot(h_ref[...], w_ref[...], preferred_element_type=F32)
    cos, sa, sb = cos_ref[...], sa_ref[...], sb_ref[...]
    for c in range(y.shape[1] // HEAD_DIM):
        sl = slice(c * HEAD_DIM, (c + 1) * HEAD_DIM)
        t = y[:, sl]
        r = t * cos + pltpu.roll(t, ROPE_HALF, 1) * sa + pltpu.roll(t, HEAD_DIM - ROPE_HALF, 1) * sb
        o_ref[:, sl] = r.astype(o_ref.dtype)


def _qkv_rope(x, g, w, li, tables, *, n_q, n_k, q_variant, tm_pref=512):
    m, d = x.shape
    n = w.shape[-1]
    tn = _tile(math.gcd(n_q, n_k, n - n_q - n_k), 1024, HEAD_DIM)
    tm = _tile(m, tm_pref, 8)
    nqb, nkb = n_q // tn, n_k // tn

    def tab_index(i, j):
        return (jnp.where(j < nqb, q_variant, jnp.where(j < nqb + nkb, ROPE_K, ROPE_NONE)), i, 0)

    tab_spec = pl.BlockSpec((None, tm, LANES), tab_index)
    vmem = 2 * tm * d * 4 + tm * d * 2 + 2 * d * tn * 2 + 2 * tm * tn * 2 + 3 * tm * tn * 4
    return pl.pallas_call(
        _qkv_rope_kernel,
        grid=(m // tm, n // tn),
        in_specs=[pl.BlockSpec((tm, d), lambda i, j: (i, 0)),
                  pl.BlockSpec((None, 1, d), lambda i, j: (li, 0, 0)),
                  pl.BlockSpec((None, d, tn), lambda i, j: (li, 0, j)),
                  tab_spec, tab_spec, tab_spec],
        out_specs=pl.BlockSpec((tm, tn), lambda i, j: (i, j)),
        out_shape=jax.ShapeDtypeStruct((m, n), BF16),
        scratch_shapes=[pltpu.VMEM((tm, d), BF16)],
        compiler_params=_params(("arbitrary", "arbitrary"), vmem),
        name="qkv_rope",
    )(x, _row(g), w, *tables)


def _matmul_residual_kernel(a_ref, w_ref, x_ref, o_ref):
    o_ref[...] = x_ref[...] + jnp.dot(a_ref[...], w_ref[...], preferred_element_type=F32)


def _matmul_residual(a, w, li, x, *, tm_pref, tn_pref):
    m, kd = a.shape
    n = w.shape[-1]
    tm, tn = _tile(m, tm_pref, 8), _tile(n, tn_pref, LANES)
    vmem = 2 * tm * kd * 2 + 2 * kd * tn * 2 + 5 * tm * tn * 4
    return pl.pallas_call(
        _matmul_residual_kernel,
        grid=(m // tm, n // tn),
        in_specs=[pl.BlockSpec((tm, kd), lambda i, j: (i, 0)),
                  pl.BlockSpec((None, kd, tn), lambda i, j: (li, 0, j)),
                  pl.BlockSpec((tm, tn), lambda i, j: (i, j))],
        out_specs=pl.BlockSpec((tm, tn), lambda i, j: (i, j)),
        out_shape=jax.ShapeDtypeStruct((m, n), F32),
        compiler_params=_params(("arbitrary", "arbitrary"), vmem),
        name="matmul_residual",
    )(a, w, x)


def _swa_kernel(sink_ref, q_ref, kc_ref, kp_ref, vc_ref, vp_ref, o_ref, *, group):
    i, h = pl.program_id(0), pl.program_id(1)
    blk = SWA_BLOCK
    nsub = q_ref.shape[0] // blk
    row = lax.broadcasted_iota(jnp.int32, (blk, 2 * blk), 0)
    col = lax.broadcasted_iota(jnp.int32, (blk, 2 * blk), 1)
    rel = col - blk
    band = (rel <= row) & (row - rel < blk)
    nt = (((1,), (1,)), ((), ()))
    for c in range(nsub):
        rows = slice(c * blk, (c + 1) * blk)
        if c == 0:
            kprev, vprev = kp_ref[...], vp_ref[...]
            mask = band & (rel >= jnp.where(i > 0, -blk, 0))
        else:
            prev = slice((c - 1) * blk, c * blk)
            kprev, vprev = kc_ref[prev, :], vc_ref[prev, :]
            mask = band
        kw = jnp.concatenate([kprev, kc_ref[rows, :]], axis=0)
        vw = jnp.concatenate([vprev, vc_ref[rows, :]], axis=0)
        qs = jnp.concatenate([q_ref[rows, g * HEAD_DIM:(g + 1) * HEAD_DIM] for g in range(group)], axis=0)
        s = lax.dot_general(qs, kw, nt, preferred_element_type=F32)
        probs = []
        for g in range(group):
            sg = jnp.where(mask, s[g * blk:(g + 1) * blk], -jnp.inf)
            sink = sink_ref[h * group + g]
            mx = jnp.maximum(jnp.max(sg, axis=-1, keepdims=True), sink)
            p = jnp.exp(sg - mx)
            den = jnp.sum(p, axis=-1, keepdims=True) + jnp.exp(sink - mx)
            probs.append((p / den).astype(BF16))
        o = jnp.dot(jnp.concatenate(probs, axis=0), vw, preferred_element_type=F32)
        for g in range(group):
            o_ref[rows, g * HEAD_DIM:(g + 1) * HEAD_DIM] = o[g * blk:(g + 1) * blk].astype(o_ref.dtype)


def _swa_attention(qkv, sinks, *, hq, hkv, tq_pref=512):
    s = qkv.shape[0]
    group = hq // hkv
    tq = _tile(s, tq_pref, SWA_BLOCK)
    r = tq // SWA_BLOCK
    cur = lambda off: pl.BlockSpec((tq, HEAD_DIM), lambda i, h, sk: (i, off + h))
    prev = lambda off: pl.BlockSpec((SWA_BLOCK, HEAD_DIM), lambda i, h, sk: (jnp.maximum(i * r - 1, 0), off + h))
    vmem = 4 * tq * group * HEAD_DIM * 2 + 8 * tq * HEAD_DIM * 2 + 8 * group * SWA_BLOCK * 2 * SWA_BLOCK * 4
    return pl.pallas_call(
        functools.partial(_swa_kernel, group=group),
        grid_spec=pltpu.PrefetchScalarGridSpec(
            num_scalar_prefetch=1,
            grid=(s // tq, hkv),
            in_specs=[pl.BlockSpec((tq, group * HEAD_DIM), lambda i, h, sk: (i, h)),
                      cur(hq), prev(hq), cur(hq + hkv), prev(hq + hkv)],
            out_specs=pl.BlockSpec((tq, group * HEAD_DIM), lambda i, h, sk: (i, h)),
        ),
        out_shape=jax.ShapeDtypeStruct((s, hq * HEAD_DIM), BF16),
        compiler_params=_params(("arbitrary", "arbitrary"), vmem),
        name="swa_attention",
    )(sinks, qkv, qkv, qkv, qkv, qkv)


def _diff_kernel(lam_ref, subln_ref, q_ref, k_ref, v_ref, o_ref, qs_ref, m_ref, l_ref, acc_ref, sa_ref, sb_ref,
                 *, group, tk, lambda_init):
    qi = pl.program_id(1)
    tq = q_ref.shape[0]
    rows = group * tq
    nt = (((1,), (1,)), ((), ()))

    for mp in range(2):
        for g in range(group):
            c = (g * 2 + mp) * HEAD_DIM
            qs_ref[mp, g * tq:(g + 1) * tq, :] = q_ref[:, c:c + HEAD_DIM]
    m_ref[...] = jnp.full(m_ref.shape, -jnp.inf, F32)
    l_ref[...] = jnp.zeros(l_ref.shape, F32)
    acc_ref[...] = jnp.zeros(acc_ref.shape, F32)

    def scores(kb, dst_ref):
        st = pl.multiple_of(kb * tk, tk)
        for mp in range(2):
            dst_ref[mp] = lax.dot_general(qs_ref[mp], k_ref[pl.ds(st, tk), mp * HEAD_DIM:(mp + 1) * HEAD_DIM], nt,
                                          preferred_element_type=F32)

    def update(kb, src_ref, masked):
        st = pl.multiple_of(kb * tk, tk)
        vblk = v_ref[pl.ds(st, tk), :]
        if masked:
            qpos = qi * tq + (lax.broadcasted_iota(jnp.int32, (rows, tk), 0) & (tq - 1))
            kpos = st + lax.broadcasted_iota(jnp.int32, (rows, tk), 1)
            visible = kpos <= qpos
        for mp in range(2):
            s = src_ref[mp]
            if masked:
                s = jnp.where(visible, s, -jnp.inf)
            m_prev = m_ref[mp]
            m_next = jnp.maximum(m_prev, jnp.max(s, axis=-1, keepdims=True))
            alpha = jnp.exp2(m_prev - m_next)
            p = jnp.exp2(s - pltpu.repeat(m_next, tk // LANES, 1))
            lane_sum = p[:, :LANES]
            for t in range(1, tk // LANES):
                lane_sum = lane_sum + p[:, t * LANES:(t + 1) * LANES]
            l_ref[mp] = alpha * l_ref[mp] + lane_sum
            pv = jnp.dot(p.astype(BF16), vblk, preferred_element_type=F32)
            acc_ref[mp] = acc_ref[mp] * pltpu.repeat(alpha, acc_ref.shape[2] // LANES, 1) + pv
            m_ref[mp] = m_next

    n_full = (qi * tq) // tk
    scores(0, sa_ref)

    def body(t, carry):
        @pl.when(t % 2 == 0)
        def _():
            scores(t + 1, sb_ref)
            update(t, sa_ref, False)

        @pl.when(t % 2 == 1)
        def _():
            scores(t + 1, sa_ref)
            update(t, sb_ref, False)

        return carry

    lax.fori_loop(0, n_full, body, 0)

    @pl.when(n_full % 2 == 0)
    def _():
        update(n_full, sa_ref, True)

    @pl.when(n_full % 2 == 1)
    def _():
        update(n_full, sb_ref, True)

    lam_vecs = lam_ref[...]
    lam = (jnp.exp(jnp.sum(lam_vecs[0:1] * lam_vecs[1:2], axis=-1, keepdims=True))
           - jnp.exp(jnp.sum(lam_vecs[2:3] * lam_vecs[3:4], axis=-1, keepdims=True)) + lambda_init)
    o = (acc_ref[0] / jnp.sum(l_ref[0], axis=-1, keepdims=True)
         - lam * (acc_ref[1] / jnp.sum(l_ref[1], axis=-1, keepdims=True)))
    y = _rms(o, subln_ref[...]) * (1.0 - lambda_init)
    vd = y.shape[1]
    for g in range(group):
        o_ref[:, g * vd:(g + 1) * vd] = y[g * tq:(g + 1) * tq].astype(o_ref.dtype)


def _diff_attention(qkv, lam_vecs, subln, li, *, heads, hkv, lambda_init, tq_pref=256, tk_pref=1024):
    s = qkv.shape[0]
    group = heads // hkv
    vd = 2 * HEAD_DIM
    tq = _tile(s, tq_pref, 8)
    tk = _tile(s, tk_pref, LANES)
    assert tk % tq == 0 and tq & (tq - 1) == 0, (tq, tk)
    rows = group * tq
    vmem = (2 * s * vd * 2 + 4 * tq * group * vd * 2 + 2 * rows * HEAD_DIM * 2
            + 4 * rows * LANES * 4 + 2 * rows * vd * 4 + 4 * rows * tk * 4 + 2 * rows * tk * 4)
    kern = functools.partial(_diff_kernel, group=group, tk=tk, lambda_init=lambda_init)
    resident = lambda off: pl.BlockSpec((s, vd), lambda h, i: (0, off + h), pipeline_mode=pl.Buffered(1))
    return pl.pallas_call(
        kern,
        grid=(hkv, s // tq),
        in_specs=[pl.BlockSpec((4, HEAD_DIM), lambda h, i: (0, 0)),
                  pl.BlockSpec((None, 1, vd), lambda h, i: (li, 0, 0)),
                  pl.BlockSpec((tq, group * vd), lambda h, i: (i, h)),
                  resident(heads), resident(heads + hkv)],
        out_specs=pl.BlockSpec((tq, group * vd), lambda h, i: (i, h)),
        out_shape=jax.ShapeDtypeStruct((s, heads * vd), BF16),
        scratch_shapes=[pltpu.VMEM((2, rows, HEAD_DIM), BF16),
                        pltpu.VMEM((2, rows, LANES), F32),
                        pltpu.VMEM((2, rows, LANES), F32),
                        pltpu.VMEM((2, rows, vd), F32),
                        pltpu.VMEM((2, rows, tk), F32),
                        pltpu.VMEM((2, rows, tk), F32)],
        compiler_params=_params(("arbitrary", "arbitrary"), vmem),
        name="diff_attention",
    )(lam_vecs, _row(subln), qkv, qkv, qkv)


def _pool_kernel(x_ref, xp_ref, g_ref, w_ref, scale_ref, o_ref, ext_ref):
    i = pl.program_id(0)
    tm = x_ref.shape[0]
    ch = w_ref.shape[1]
    x = x_ref[...]
    h = _rms(x, g_ref[...])
    hp = _rms(xp_ref[...], g_ref[...])
    ext_ref[0:POOL_HALO, :] = jnp.where(i > 0, hp, 0.0)
    ext_ref[POOL_HALO:, :] = h
    t = i * tm + lax.broadcasted_iota(jnp.int32, (tm, 1), 0)
    for gi, win in enumerate(POOL_WINDOWS):
        cols = slice(gi * ch, (gi + 1) * ch)
        e = ext_ref[:, cols]
        span = 1
        while span < win:
            e = e + pltpu.roll(e, span, 0)
            span *= 2
        count = jnp.minimum(t + 1, win).astype(F32)
        pooled = (e[POOL_HALO:] / count - h[:, cols]).astype(BF16)
        y = jnp.dot(pooled, w_ref[gi], preferred_element_type=F32)
        o_ref[:, cols] = x[:, cols] + y * scale_ref[:, cols]


def _pool_mixer(x, g, w, scale, li, *, tm_pref=256):
    s, d = x.shape
    groups, ch = w.shape[1], w.shape[2]
    assert groups == len(POOL_WINDOWS)
    tm = _tile(s, tm_pref, POOL_HALO)
    r = tm // POOL_HALO
    vec = pl.BlockSpec((None, 1, d), lambda i: (li, 0, 0))
    vmem = 4 * tm * d * 4 + 2 * groups * ch * ch * 2 + (tm + POOL_HALO) * d * 4 + 6 * tm * ch * 4
    return pl.pallas_call(
        _pool_kernel,
        grid=(s // tm,),
        in_specs=[pl.BlockSpec((tm, d), lambda i: (i, 0)),
                  pl.BlockSpec((POOL_HALO, d), lambda i: (jnp.maximum(i * r - 1, 0), 0)),
                  vec,
                  pl.BlockSpec((None, groups, ch, ch), lambda i: (li, 0, 0, 0)),
                  vec],
        out_specs=pl.BlockSpec((tm, d), lambda i: (i, 0)),
        out_shape=jax.ShapeDtypeStruct((s, d), F32),
        scratch_shapes=[pltpu.VMEM((tm + POOL_HALO, d), F32)],
        compiler_params=_params(("arbitrary",), vmem),
        name="pool_mixer",
    )(x, x, _row(g), w, _row(scale))


def _xattn_kernel(x_ref, g_ref, wq_ref, kv_ref, wo_ref, gn_ref, o_ref, hn_ref, *, heads):
    x = x_ref[...]
    h = _rms(x, g_ref[...]).astype(BF16)
    width = wq_ref.shape[1]
    hd = width // heads
    q = (jnp.dot(h, wq_ref[...], preferred_element_type=F32) * (hd ** -0.5)).astype(BF16)
    nt = (((1,), (1,)), ((), ()))
    outs = []
    for a in range(heads):
        cols = slice(a * hd, (a + 1) * hd)
        s = lax.dot_general(q[:, cols], kv_ref[:, cols], nt, preferred_element_type=F32)
        p = jnp.exp(s - jnp.max(s, axis=-1, keepdims=True))
        den = jnp.sum(p, axis=-1, keepdims=True)
        pv = jnp.dot(p.astype(BF16), kv_ref[:, width + a * hd:width + (a + 1) * hd], preferred_element_type=F32)
        outs.append((pv / den).astype(BF16))
    o = jnp.concatenate(outs, axis=-1)
    y = x + jnp.dot(o, wo_ref[...], preferred_element_type=F32)
    o_ref[...] = y
    hn_ref[...] = _rms(y, gn_ref[...]).astype(hn_ref.dtype)


def _xattn(x, g, wq, kv, wo, g_next, li, *, tm_pref=256):
    s, d = x.shape
    width = wq.shape[-1]
    mem_len = kv.shape[0]
    tm = _tile(s, tm_pref, 8)
    vmem = (4 * tm * d * 4 + 2 * tm * d * 2 + 2 * d * width * 2 + mem_len * 2 * width * 2 + tm * d * 6
            + 4 * tm * width * 4)
    layer = lambda shape: pl.BlockSpec((None,) + shape, lambda i: (li, 0, 0), pipeline_mode=pl.Buffered(1))
    rows = pl.BlockSpec((tm, d), lambda i: (i, 0))
    return pl.pallas_call(
        functools.partial(_xattn_kernel, heads=X_HEADS),
        grid=(s // tm,),
        in_specs=[rows, layer((1, d)), layer((d, width)),
                  pl.BlockSpec((mem_len, 2 * width), lambda i: (0, 0), pipeline_mode=pl.Buffered(1)),
                  layer((width, d)), layer((1, d))],
        out_specs=[rows, rows],
        out_shape=[jax.ShapeDtypeStruct((s, d), F32), jax.ShapeDtypeStruct((s, d), BF16)],
        compiler_params=_params(("arbitrary",), vmem),
        name="xattn",
    )(x, _row(g), wq, kv, wo, _row(g_next))


def _ffn_in_kernel(h_ref, wg_ref, wv_ref, cwg_ref, cwv_ref, cbg_ref, cbv_ref, o_ref, cg_ref, cv_ref, *, chunk):
    i, j = pl.program_id(0), pl.program_id(1)
    tm = h_ref.shape[0]

    @pl.when(i == 0)
    def _():
        cg_ref[j] = jnp.zeros(cg_ref.shape[1:], F32)
        cv_ref[j] = jnp.zeros(cv_ref.shape[1:], F32)

    def conv(w_ref, cw_ref, cb_ref, carry_ref):
        u = jnp.dot(h_ref[...], w_ref[...], preferred_element_type=F32)
        first = carry_ref[j]
        carry_ref[j] = u[tm - CONV_HALO:, :]
        out = []
        for r0 in range(0, tm, chunk):
            prev = first if r0 == 0 else u[r0 - CONV_HALO:r0, :]
            ext = jnp.concatenate([prev, u[r0:r0 + chunk, :]], axis=0)
            c = cb_ref[...]
            for tap in range(CONV_WIDTH):
                lag = CONV_WIDTH - 1 - tap
                shifted = pltpu.roll(ext, lag, 0) if lag else ext
                c = c + cw_ref[tap:tap + 1, :] * shifted[CONV_HALO:, :]
            out.append(c)
        return out

    gate = conv(wg_ref, cwg_ref, cbg_ref, cg_ref)
    val = conv(wv_ref, cwv_ref, cbv_ref, cv_ref)
    for k, (gk, vk) in enumerate(zip(gate, val)):
        o_ref[k * chunk:(k + 1) * chunk, :] = (gk / (1.0 + jnp.exp(-gk)) * vk).astype(o_ref.dtype)


def _ffn_in(h, w_in, conv_w, conv_b, li, *, tm_pref=1024, tn_pref=512, chunk_pref=256):
    m, d = h.shape
    f = w_in.shape[-1] // 2
    tm, tn = _tile(m, tm_pref, 8), _tile(f, tn_pref, LANES)
    chunk = _tile(tm, chunk_pref, 8)
    nj = f // tn
    lo = lambda rows: pl.BlockSpec((None, rows, tn), lambda i, j: (li, 0, j))
    hi = lambda rows: pl.BlockSpec((None, rows, tn), lambda i, j: (li, 0, j + nj))
    vmem = 2 * tm * d * 2 + 4 * d * tn * 2 + 2 * tm * tn * 2 + 2 * nj * CONV_HALO * tn * 4 + 6 * tm * tn * 4
    return pl.pallas_call(
        functools.partial(_ffn_in_kernel, chunk=chunk),
        grid=(m // tm, nj),
        in_specs=[pl.BlockSpec((tm, d), lambda i, j: (i, 0)),
                  lo(d), hi(d), lo(CONV_WIDTH), hi(CONV_WIDTH), lo(1), hi(1)],
        out_specs=pl.BlockSpec((tm, tn), lambda i, j: (i, j)),
        out_shape=jax.ShapeDtypeStruct((m, f), BF16),
        scratch_shapes=[pltpu.VMEM((nj, CONV_HALO, tn), F32),
                        pltpu.VMEM((nj, CONV_HALO, tn), F32)],
        compiler_params=_params(("arbitrary", "arbitrary"), vmem),
        name="ffn_in",
    )(h, w_in, w_in, conv_w, conv_w, _row(conv_b), _row(conv_b))


def _final_norm_kernel(x_ref, g_ref, o_ref):
    o_ref[...] = _rms(x_ref[...], g_ref[...])


def _final_norm(x, g, *, tm_pref=512):
    s, d = x.shape
    tm = _tile(s, tm_pref, 8)
    return pl.pallas_call(
        _final_norm_kernel,
        grid=(s // tm,),
        in_specs=[pl.BlockSpec((tm, d), lambda i: (i, 0)), pl.BlockSpec((1, d), lambda i: (0, 0))],
        out_specs=pl.BlockSpec((tm, d), lambda i: (i, 0)),
        out_shape=jax.ShapeDtypeStruct((s, d), F32),
        compiler_params=_params(("arbitrary",), 6 * tm * d * 4),
        name="final_norm",
    )(x, g.reshape(1, d))


def kernel(x, mem, positions, swa_norm, swa_w_qkv, swa_sinks, swa_w_o, diff_norm, diff_w_qkv, diff_lambda_q1, diff_lambda_k1, diff_lambda_q2, diff_lambda_k2, diff_subln, diff_w_o, pool_norm, pool_w, pool_scale, xattn_norm, xattn_mem_norm, xattn_w_q, xattn_w_kv, xattn_w_o, ffn_norm, ffn_w_in, ffn_conv_w, ffn_conv_b, ffn_w_out, final_norm):
    batch, seq, d = x.shape
    assert batch == 1, "the kernels fold no batch dimension"
    depth = ffn_w_in.shape[0]
    xs = x.reshape(seq, d)
    mem2 = mem.reshape(mem.shape[1], d)
    tables = _rope_tables(positions)
    bf = lambda w: w.astype(BF16)
    swa_w_qkv, swa_w_o, diff_w_qkv, diff_w_o, pool_w = map(bf, (swa_w_qkv, swa_w_o, diff_w_qkv, diff_w_o, pool_w))
    xattn_w_q, xattn_w_kv, xattn_w_o, ffn_w_in, ffn_w_out = map(
        bf, (xattn_w_q, xattn_w_kv, xattn_w_o, ffn_w_in, ffn_w_out))

    swa_hq = d // HEAD_DIM
    swa_hkv = (swa_w_qkv.shape[-1] // HEAD_DIM - swa_hq) // 2
    diff_heads = d // (2 * HEAD_DIM)
    diff_hkv = (diff_w_qkv.shape[-1] // (2 * HEAD_DIM) - diff_heads) // 2

    for i in range(depth):
        kind, j = i % N_MIXERS, i // N_MIXERS
        if kind == 0:
            qkv = _qkv_rope(xs, swa_norm, swa_w_qkv, j, tables, n_q=swa_hq * HEAD_DIM, n_k=swa_hkv * HEAD_DIM,
                            q_variant=ROPE_Q_SWA)
            att = _swa_attention(qkv, swa_sinks[j], hq=swa_hq, hkv=swa_hkv)
            xs = _matmul_residual(att, swa_w_o, j, xs, tm_pref=1024, tn_pref=512)
        elif kind == 1:
            lambda_init = 0.8 - 0.6 * math.exp(-0.3 * i)
            qkv = _qkv_rope(xs, diff_norm, diff_w_qkv, j, tables, n_q=2 * diff_heads * HEAD_DIM,
                            n_k=2 * diff_hkv * HEAD_DIM, q_variant=ROPE_Q_DIFF)
            lam_vecs = jnp.stack([diff_lambda_q1[j], diff_lambda_k1[j], diff_lambda_q2[j], diff_lambda_k2[j]])
            att = _diff_attention(qkv, lam_vecs, diff_subln, j, heads=diff_heads, hkv=diff_hkv,
                                  lambda_init=lambda_init)
            xs = _matmul_residual(att, diff_w_o, j, xs, tm_pref=1024, tn_pref=512)
        else:
            xs = _pool_mixer(xs, pool_norm, pool_w, pool_scale, j)
        kv = _norm_matmul(mem2, xattn_mem_norm, xattn_w_kv, i)
        xs, hn = _xattn(xs, xattn_norm, xattn_w_q, kv, xattn_w_o, ffn_norm, i)
        act = _ffn_in(hn, ffn_w_in, ffn_conv_w, ffn_conv_b, i)
        xs = _matmul_residual(act, ffn_w_out, i, xs, tm_pref=512, tn_pref=256)
    return _final_norm(xs, final_norm).reshape(batch, seq, d)
```

```python
import functools
import math

import jax
import jax.numpy as jnp
from jax import lax
from jax.experimental import pallas as pl
from jax.experimental.pallas import tpu as pltpu

HEAD_DIM = 128
ROPE_DIM = HEAD_DIM // 4
ROPE_HALF = ROPE_DIM // 2
ROPE_THETA = 500000.0
SWA_BLOCK = 128
POOL_WINDOWS = (2, 4, 8, 16)
POOL_HALO = 16
X_HEADS = 4
CONV_WIDTH = 3
CONV_HALO = 8
N_MIXERS = 3
NORM_EPS = 1e-5
LANES = 128
LOG2E = math.log2(math.e)

V7X_VMEM_BYTES = 64 * 2**20
VMEM_RESERVE_BYTES = 6 * 2**20

F32 = jnp.float32
BF16 = jnp.bfloat16

ROPE_Q_SWA, ROPE_Q_DIFF, ROPE_K, ROPE_NONE = 0, 1, 2, 3
ROPE_SCALES = (HEAD_DIM ** -0.5, HEAD_DIM ** -0.5 * LOG2E, 1.0)


def _params(semantics, vmem_bytes, flags=None):
    limit = min(int(vmem_bytes) * 5 // 4 + VMEM_RESERVE_BYTES, V7X_VMEM_BYTES - VMEM_RESERVE_BYTES // 2)
    return pltpu.CompilerParams(dimension_semantics=semantics, vmem_limit_bytes=limit, flags=flags)


def _tile(dim, pref, align):
    if dim <= pref:
        return dim
    t = (pref // align) * align
    while t > align and dim % t:
        t -= align
    assert dim % t == 0, (dim, pref, align)
    return t


def _rms(x, g, eps=NORM_EPS):
    ms = jnp.mean(x * x, axis=-1, keepdims=True)
    return x * lax.rsqrt(ms + eps) * g


def _row(stack):
    return stack.reshape(stack.shape[0], 1, stack.shape[-1])


def _rope_table_kernel(pos_ref, invf_ref, cos_ref, sa_ref, sb_ref):
    ang = pos_ref[...].astype(F32) * invf_ref[...]
    lane = lax.broadcasted_iota(jnp.int32, ang.shape, 1)
    s = jnp.sin(ang)
    cos = jnp.cos(ang)
    sa = jnp.where(lane >= ROPE_HALF, s, 0.0)
    sb = jnp.where(lane < ROPE_HALF, -s, 0.0)
    for v, scale in enumerate(ROPE_SCALES):
        cos_ref[v], sa_ref[v], sb_ref[v] = cos * scale, sa * scale, sb * scale
    cos_ref[ROPE_NONE] = jnp.ones(cos.shape, F32)
    sa_ref[ROPE_NONE] = jnp.zeros(cos.shape, F32)
    sb_ref[ROPE_NONE] = jnp.zeros(cos.shape, F32)


def _rope_tables(positions):
    s = positions.shape[-1]
    inv_freq = ROPE_THETA ** (-jnp.arange(0, ROPE_DIM, 2, dtype=F32) / ROPE_DIM)
    invf = jnp.zeros((1, LANES), F32).at[0, :ROPE_DIM].set(jnp.concatenate([inv_freq, inv_freq]))
    nv = len(ROPE_SCALES) + 1
    tm = _tile(s, 1024, 8)
    tab = jax.ShapeDtypeStruct((nv, s, LANES), F32)
    return pl.pallas_call(
        _rope_table_kernel,
        grid=(s // tm,),
        in_specs=[pl.BlockSpec((tm, 1), lambda i: (i, 0)), pl.BlockSpec((1, LANES), lambda i: (0, 0))],
        out_specs=[pl.BlockSpec((nv, tm, LANES), lambda i: (0, i, 0))] * 3,
        out_shape=[tab] * 3,
        compiler_params=_params(("arbitrary",), (6 * nv + 8) * tm * LANES * 4),
        name="rope_tables",
    )(positions.reshape(s, 1), invf)


def _norm_to_scratch(x_ref, g_ref, h_ref):
    @pl.when(pl.program_id(1) == 0)
    def _():
        h_ref[...] = _rms(x_ref[...], g_ref[...]).astype(h_ref.dtype)


def _norm_matmul_kernel(x_ref, g_ref, w_ref, o_ref, h_ref):
    _norm_to_scratch(x_ref, g_ref, h_ref)
    o_ref[...] = jnp.dot(h_ref[...], w_ref[...], preferred_element_type=F32).astype(o_ref.dtype)


def _norm_matmul(x, g, w, li, *, tm_pref=512, tn_pref=1024):
    m, d = x.shape
    n = w.shape[-1]
    tm, tn = _tile(m, tm_pref, 8), _tile(n, tn_pref, LANES)
    vmem = 2 * tm * d * 4 + tm * d * 2 + 2 * d * tn * 2 + 2 * tm * tn * 2 + 2 * tm * tn * 4
    return pl.pallas_call(
        _norm_matmul_kernel,
        grid=(m // tm, n // tn),
        in_specs=[pl.BlockSpec((tm, d), lambda i, j: (i, 0)),
                  pl.BlockSpec((None, 1, d), lambda i, j: (li, 0, 0)),
                  pl.BlockSpec((None, d, tn), lambda i, j: (li, 0, j))],
        out_specs=pl.BlockSpec((tm, tn), lambda i, j: (i, j)),
        out_shape=jax.ShapeDtypeStruct((m, n), BF16),
        scratch_shapes=[pltpu.VMEM((tm, d), BF16)],
        compiler_params=_params(("arbitrary", "arbitrary"), vmem),
        name="norm_matmul",
    )(x, _row(g), w)


def _qkv_rope_kernel(x_ref, g_ref, w_ref, cos_ref, sa_ref, sb_ref, o_ref, h_ref):
    _norm_to_scratch(x_ref, g_ref, h_ref)
    y = jnp.dot(h_ref[...], w_ref[...], preferred_element_type=F32)
    cos, sa, sb = cos_ref[...], sa_ref[...], sb_ref[...]
    for c in range(y.shape[1] // HEAD_DIM):
        sl = slice(c * HEAD_DIM, (c + 1) * HEAD_DIM)
        t = y[:, sl]
        r = t * cos + pltpu.roll(t, ROPE_HALF, 1) * sa + pltpu.roll(t, HEAD_DIM - ROPE_HALF, 1) * sb
        o_ref[:, sl] = r.astype(o_ref.dtype)


def _qkv_rope(x, g, w, li, tables, *, n_q, n_k, q_variant, tm_pref=512):
    m, d = x.shape
    n = w.shape[-1]
    tn = _tile(math.gcd(n_q, n_k, n - n_q - n_k), 1024, HEAD_DIM)
    tm = _tile(m, tm_pref, 8)
    nqb, nkb = n_q // tn, n_k // tn

    def tab_index(i, j):
        return (jnp.where(j < nqb, q_variant, jnp.where(j < nqb + nkb, ROPE_K, ROPE_NONE)), i, 0)

    tab_spec = pl.BlockSpec((None, tm, LANES), tab_index)
    vmem = 2 * tm * d * 4 + tm * d * 2 + 2 * d * tn * 2 + 2 * tm * tn * 2 + 3 * tm * tn * 4
    return pl.pallas_call(
        _qkv_rope_kernel,
        grid=(m // tm, n // tn),
        in_specs=[pl.BlockSpec((tm, d), lambda i, j: (i, 0)),
                  pl.BlockSpec((None, 1, d), lambda i, j: (li, 0, 0)),
                  pl.BlockSpec((None, d, tn), lambda i, j: (li, 0, j)),
                  tab_spec, tab_spec, tab_spec],
        out_specs=pl.BlockSpec((tm, tn), lambda i, j: (i, j)),
        out_shape=jax.ShapeDtypeStruct((m, n), BF16),
        scratch_shapes=[pltpu.VMEM((tm, d), BF16)],
        compiler_params=_params(("arbitrary", "arbitrary"), vmem),
        name="qkv_rope",
    )(x, _row(g), w, *tables)


def _matmul_residual_kernel(a_ref, w_ref, x_ref, o_ref):
    o_ref[...] = x_ref[...] + jnp.dot(a_ref[...], w_ref[...], preferred_element_type=F32)


def _matmul_residual(a, w, li, x, *, tm_pref, tn_pref):
    m, kd = a.shape
    n = w.shape[-1]
    tm, tn = _tile(m, tm_pref, 8), _tile(n, tn_pref, LANES)
    vmem = 2 * tm * kd * 2 + 2 * kd * tn * 2 + 5 * tm * tn * 4
    return pl.pallas_call(
        _matmul_residual_kernel,
        grid=(m // tm, n // tn),
        in_specs=[pl.BlockSpec((tm, kd), lambda i, j: (i, 0)),
                  pl.BlockSpec((None, kd, tn), lambda i, j: (li, 0, j)),
                  pl.BlockSpec((tm, tn), lambda i, j: (i, j))],
        out_specs=pl.BlockSpec((tm, tn), lambda i, j: (i, j)),
        out_shape=jax.ShapeDtypeStruct((m, n), F32),
        compiler_params=_params(("arbitrary", "arbitrary"), vmem),
        name="matmul_residual",
    )(a, w, x)


def _swa_kernel(sink_ref, q_ref, kc_ref, kp_ref, vc_ref, vp_ref, o_ref, *, group):
    i, h = pl.program_id(0), pl.program_id(1)
    blk = SWA_BLOCK
    nsub = q_ref.shape[0] // blk
    row = lax.broadcasted_iota(jnp.int32, (blk, 2 * blk), 0)
    col = lax.broadcasted_iota(jnp.int32, (blk, 2 * blk), 1)
    rel = col - blk
    band = (rel <= row) & (row - rel < blk)
    nt = (((1,), (1,)), ((), ()))
    for c in range(nsub):
        rows = slice(c * blk, (c + 1) * blk)
        if c == 0:
            kprev, vprev = kp_ref[...], vp_ref[...]
            mask = band & (rel >= jnp.where(i > 0, -blk, 0))
        else:
            prev = slice((c - 1) * blk, c * blk)
            kprev, vprev = kc_ref[prev, :], vc_ref[prev, :]
            mask = band
        kw = jnp.concatenate([kprev, kc_ref[rows, :]], axis=0)
        vw = jnp.concatenate([vprev, vc_ref[rows, :]], axis=0)
        qs = jnp.concatenate([q_ref[rows, g * HEAD_DIM:(g + 1) * HEAD_DIM] for g in range(group)], axis=0)
        s = lax.dot_general(qs, kw, nt, preferred_element_type=F32)
        probs = []
        for g in range(group):
            sg = jnp.where(mask, s[g * blk:(g + 1) * blk], -jnp.inf)
            sink = sink_ref[h * group + g]
            mx = jnp.maximum(jnp.max(sg, axis=-1, keepdims=True), sink)
            p = jnp.exp(sg - mx)
            den = jnp.sum(p, axis=-1, keepdims=True) + jnp.exp(sink - mx)
            probs.append((p / den).astype(BF16))
        o = jnp.dot(jnp.concatenate(probs, axis=0), vw, preferred_element_type=F32)
        for g in range(group):
            o_ref[rows, g * HEAD_DIM:(g + 1) * HEAD_DIM] = o[g * blk:(g + 1) * blk].astype(o_ref.dtype)


def _swa_attention(qkv, sinks, *, hq, hkv, tq_pref=512):
    s = qkv.shape[0]
    group = hq // hkv
    tq = _tile(s, tq_pref, SWA_BLOCK)
    r = tq // SWA_BLOCK
    cur = lambda off: pl.BlockSpec((tq, HEAD_DIM), lambda i, h, sk: (i, off + h))
    prev = lambda off: pl.BlockSpec((SWA_BLOCK, HEAD_DIM), lambda i, h, sk: (jnp.maximum(i * r - 1, 0), off + h))
    vmem = 4 * tq * group * HEAD_DIM * 2 + 8 * tq * HEAD_DIM * 2 + 8 * group * SWA_BLOCK * 2 * SWA_BLOCK * 4
    return pl.pallas_call(
        functools.partial(_swa_kernel, group=group),
        grid_spec=pltpu.PrefetchScalarGridSpec(
            num_scalar_prefetch=1,
            grid=(s // tq, hkv),
            in_specs=[pl.BlockSpec((tq, group * HEAD_DIM), lambda i, h, sk: (i, h)),
                      cur(hq), prev(hq), cur(hq + hkv), prev(hq + hkv)],
            out_specs=pl.BlockSpec((tq, group * HEAD_DIM), lambda i, h, sk: (i, h)),
        ),
        out_shape=jax.ShapeDtypeStruct((s, hq * HEAD_DIM), BF16),
        compiler_params=_params(("arbitrary", "arbitrary"), vmem),
        name="swa_attention",
    )(sinks, qkv, qkv, qkv, qkv, qkv)


def _diff_kernel(lam_ref, subln_ref, q_ref, k_ref, v_ref, o_ref, qs_ref, m_ref, l_ref, acc_ref, sa_ref, sb_ref,
                 *, group, tk, lambda_init):
    qi = pl.program_id(1)
    tq = q_ref.shape[0]
    rows = group * tq
    nt = (((1,), (1,)), ((), ()))

    for mp in range(2):
        for g in range(group):
            c = (g * 2 + mp) * HEAD_DIM
            qs_ref[mp, g * tq:(g + 1) * tq, :] = q_ref[:, c:c + HEAD_DIM]
    m_ref[...] = jnp.full(m_ref.shape, -jnp.inf, F32)
    l_ref[...] = jnp.zeros(l_ref.shape, F32)
    acc_ref[...] = jnp.zeros(acc_ref.shape, F32)

    def scores(kb, dst_ref):
        st = pl.multiple_of(kb * tk, tk)
        for mp in range(2):
            dst_ref[mp] = lax.dot_general(qs_ref[mp], k_ref[pl.ds(st, tk), mp * HEAD_DIM:(mp + 1) * HEAD_DIM], nt,
                                          preferred_element_type=F32)

    def update(kb, src_ref, masked):
        st = pl.multiple_of(kb * tk, tk)
        vblk = v_ref[pl.ds(st, tk), :]
        if masked:
            qpos = qi * tq + (lax.broadcasted_iota(jnp.int32, (rows, tk), 0) & (tq - 1))
            kpos = st + lax.broadcasted_iota(jnp.int32, (rows, tk), 1)
            visible = kpos <= qpos
        for mp in range(2):
            s = src_ref[mp]
            if masked:
                s = jnp.where(visible, s, -jnp.inf)
            m_prev = m_ref[mp]
            m_next = jnp.maximum(m_prev, jnp.max(s, axis=-1, keepdims=True))
            alpha = jnp.exp2(m_prev - m_next)
            p = jnp.exp2(s - pltpu.repeat(m_next, tk // LANES, 1))
            lane_sum = p[:, :LANES]
            for t in range(1, tk // LANES):
                lane_sum = lane_sum + p[:, t * LANES:(t + 1) * LANES]
            l_ref[mp] = alpha * l_ref[mp] + lane_sum
            pv = jnp.dot(p.astype(BF16), vblk, preferred_element_type=F32)
            acc_ref[mp] = acc_ref[mp] * pltpu.repeat(alpha, acc_ref.shape[2] // LANES, 1) + pv
            m_ref[mp] = m_next

    n_full = (qi * tq) // tk
    scores(0, sa_ref)

    def body(t, carry):
        @pl.when(t % 2 == 0)
        def _():
            scores(t + 1, sb_ref)
            update(t, sa_ref, False)

        @pl.when(t % 2 == 1)
        def _():
            scores(t + 1, sa_ref)
            update(t, sb_ref, False)

        return carry

    lax.fori_loop(0, n_full, body, 0)

    @pl.when(n_full % 2 == 0)
    def _():
        update(n_full, sa_ref, True)

    @pl.when(n_full % 2 == 1)
    def _():
        update(n_full, sb_ref, True)

    lam_vecs = lam_ref[...]
    lam = (jnp.exp(jnp.sum(lam_vecs[0:1] * lam_vecs[1:2], axis=-1, keepdims=True))
           - jnp.exp(jnp.sum(lam_vecs[2:3] * lam_vecs[3:4], axis=-1, keepdims=True)) + lambda_init)
    o = (acc_ref[0] / jnp.sum(l_ref[0], axis=-1, keepdims=True)
         - lam * (acc_ref[1] / jnp.sum(l_ref[1], axis=-1, keepdims=True)))
    y = _rms(o, subln_ref[...]) * (1.0 - lambda_init)
    vd = y.shape[1]
    for g in range(group):
        o_ref[:, g * vd:(g + 1) * vd] = y[g * tq:(g + 1) * tq].astype(o_ref.dtype)


def _diff_attention(qkv, lam_vecs, subln, li, *, heads, hkv, lambda_init, tq_pref=256, tk_pref=1024):
    s = qkv.shape[0]
    group = heads // hkv
    vd = 2 * HEAD_DIM
    tq = _tile(s, tq_pref, 8)
    tk = _tile(s, tk_pref, LANES)
    assert tk % tq == 0 and tq & (tq - 1) == 0, (tq, tk)
    rows = group * tq
    vmem = (2 * s * vd * 2 + 4 * tq * group * vd * 2 + 2 * rows * HEAD_DIM * 2
            + 4 * rows * LANES * 4 + 2 * rows * vd * 4 + 4 * rows * tk * 4 + 2 * rows * tk * 4)
    kern = functools.partial(_diff_kernel, group=group, tk=tk, lambda_init=lambda_init)
    resident = lambda off: pl.BlockSpec((s, vd), lambda h, i: (0, off + h), pipeline_mode=pl.Buffered(1))
    return pl.pallas_call(
        kern,
        grid=(hkv, s // tq),
        in_specs=[pl.BlockSpec((4, HEAD_DIM), lambda h, i: (0, 0)),
                  pl.BlockSpec((None, 1, vd), lambda h, i: (li, 0, 0)),
                  pl.BlockSpec((tq, group * vd), lambda h, i: (i, h)),
                  resident(heads), resident(heads + hkv)],
        out_specs=pl.BlockSpec((tq, group * vd), lambda h, i: (i, h)),
        out_shape=jax.ShapeDtypeStruct((s, heads * vd), BF16),
        scratch_shapes=[pltpu.VMEM((2, rows, HEAD_DIM), BF16),
                        pltpu.VMEM((2, rows, LANES), F32),
                        pltpu.VMEM((2, rows, LANES), F32),
                        pltpu.VMEM((2, rows, vd), F32),
                        pltpu.VMEM((2, rows, tk), F32),
                        pltpu.VMEM((2, rows, tk), F32)],
        compiler_params=_params(("arbitrary", "arbitrary"), vmem),
        name="diff_attention",
    )(lam_vecs, _row(subln), qkv, qkv, qkv)


def _pool_kernel(x_ref, xp_ref, g_ref, w_ref, scale_ref, o_ref, ext_ref):
    i = pl.program_id(0)
    tm = x_ref.shape[0]
    ch = w_ref.shape[1]
    x = x_ref[...]
    h = _rms(x, g_ref[...])
    hp = _rms(xp_ref[...], g_ref[...])
    ext_ref[0:POOL_HALO, :] = jnp.where(i > 0, hp, 0.0)
    ext_ref[POOL_HALO:, :] = h
    t = i * tm + lax.broadcasted_iota(jnp.int32, (tm, 1), 0)
    for gi, win in enumerate(POOL_WINDOWS):
        cols = slice(gi * ch, (gi + 1) * ch)
        e = ext_ref[:, cols]
        span = 1
        while span < win:
            e = e + pltpu.roll(e, span, 0)
            span *= 2
        count = jnp.minimum(t + 1, win).astype(F32)
        pooled = (e[POOL_HALO:] / count - h[:, cols]).astype(BF16)
        y = jnp.dot(pooled, w_ref[gi], preferred_element_type=F32)
        o_ref[:, cols] = x[:, cols] + y * scale_ref[:, cols]


def _pool_mixer(x, g, w, scale, li, *, tm_pref=256):
    s, d = x.shape
    groups, ch = w.shape[1], w.shape[2]
    assert groups == len(POOL_WINDOWS)
    tm = _tile(s, tm_pref, POOL_HALO)
    r = tm // POOL_HALO
    vec = pl.BlockSpec((None, 1, d), lambda i: (li, 0, 0))
    vmem = 4 * tm * d * 4 + 2 * groups * ch * ch * 2 + (tm + POOL_HALO) * d * 4 + 6 * tm * ch * 4
    return pl.pallas_call(
        _pool_kernel,
        grid=(s // tm,),
        in_specs=[pl.BlockSpec((tm, d), lambda i: (i, 0)),
                  pl.BlockSpec((POOL_HALO, d), lambda i: (jnp.maximum(i * r - 1, 0), 0)),
                  vec,
                  pl.BlockSpec((None, groups, ch, ch), lambda i: (li, 0, 0, 0)),
                  vec],
        out_specs=pl.BlockSpec((tm, d), lambda i: (i, 0)),
        out_shape=jax.ShapeDtypeStruct((s, d), F32),
        scratch_shapes=[pltpu.VMEM((tm + POOL_HALO, d), F32)],
        compiler_params=_params(("arbitrary",), vmem),
        name="pool_mixer",
    )(x, x, _row(g), w, _row(scale))


def _xattn_kernel(x_ref, g_ref, wq_ref, kv_ref, wo_ref, gn_ref, o_ref, hn_ref, *, heads):
    x = x_ref[...]
    h = _rms(x, g_ref[...]).astype(BF16)
    width = wq_ref.shape[1]
    hd = width // heads
    q = (jnp.dot(h, wq_ref[...], preferred_element_type=F32) * (hd ** -0.5)).astype(BF16)
    nt = (((1,), (1,)), ((), ()))
    outs = []
    for a in range(heads):
        cols = slice(a * hd, (a + 1) * hd)
        s = lax.dot_general(q[:, cols], kv_ref[:, cols], nt, preferred_element_type=F32)
        p = jnp.exp(s - jnp.max(s, axis=-1, keepdims=True))
        den = jnp.sum(p, axis=-1, keepdims=True)
        pv = jnp.dot(p.astype(BF16), kv_ref[:, width + a * hd:width + (a + 1) * hd], preferred_element_type=F32)
        outs.append((pv / den).astype(BF16))
    o = jnp.concatenate(outs, axis=-1)
    y = x + jnp.dot(o, wo_ref[...], preferred_element_type=F32)
    o_ref[...] = y
    hn_ref[...] = _rms(y, gn_ref[...]).astype(hn_ref.dtype)


def _xattn(x, g, wq, kv, wo, g_next, li, *, tm_pref=256):
    s, d = x.shape
    width = wq.shape[-1]
    mem_len = kv.shape[0]
    tm = _tile(s, tm_pref, 8)
    vmem = (4 * tm * d * 4 + 2 * tm * d * 2 + 2 * d * width * 2 + mem_len * 2 * width * 2 + tm * d * 6
            + 4 * tm * width * 4)
    layer = lambda shape: pl.BlockSpec((None,) + shape, lambda i: (li, 0, 0), pipeline_mode=pl.Buffered(1))
    rows = pl.BlockSpec((tm, d), lambda i: (i, 0))
    return pl.pallas_call(
        functools.partial(_xattn_kernel, heads=X_HEADS),
        grid=(s // tm,),
        in_specs=[rows, layer((1, d)), layer((d, width)),
                  pl.BlockSpec((mem_len, 2 * width), lambda i: (0, 0), pipeline_mode=pl.Buffered(1)),
                  layer((width, d)), layer((1, d))],
        out_specs=[rows, rows],
        out_shape=[jax.ShapeDtypeStruct((s, d), F32), jax.ShapeDtypeStruct((s, d), BF16)],
        compiler_params=_params(("arbitrary",), vmem),
        name="xattn",
    )(x, _row(g), wq, kv, wo, _row(g_next))


def _ffn_in_kernel(h_ref, wg_ref, wv_ref, cwg_ref, cwv_ref, cbg_ref, cbv_ref, o_ref, cg_ref, cv_ref, *, chunk):
    i, j = pl.program_id(0), pl.program_id(1)
    tm = h_ref.shape[0]

    @pl.when(i == 0)
    def _():
        cg_ref[j] = jnp.zeros(cg_ref.shape[1:], F32)
        cv_ref[j] = jnp.zeros(cv_ref.shape[1:], F32)

    def conv(w_ref, cw_ref, cb_ref, carry_ref):
        u = jnp.dot(h_ref[...], w_ref[...], preferred_element_type=F32)
        first = carry_ref[j]
        carry_ref[j] = u[tm - CONV_HALO:, :]
        out = []
        for r0 in range(0, tm, chunk):
            prev = first if r0 == 0 else u[r0 - CONV_HALO:r0, :]
            ext = jnp.concatenate([prev, u[r0:r0 + chunk, :]], axis=0)
            c = cb_ref[...]
            for tap in range(CONV_WIDTH):
                lag = CONV_WIDTH - 1 - tap
                shifted = pltpu.roll(ext, lag, 0) if lag else ext
                c = c + cw_ref[tap:tap + 1, :] * shifted[CONV_HALO:, :]
            out.append(c)
        return out

    gate = conv(wg_ref, cwg_ref, cbg_ref, cg_ref)
    val = conv(wv_ref, cwv_ref, cbv_ref, cv_ref)
    for k, (gk, vk) in enumerate(zip(gate, val)):
        o_ref[k * chunk:(k + 1) * chunk, :] = (gk / (1.0 + jnp.exp(-gk)) * vk).astype(o_ref.dtype)


def _ffn_in(h, w_in, conv_w, conv_b, li, *, tm_pref=1024, tn_pref=512, chunk_pref=256):
    m, d = h.shape
    f = w_in.shape[-1] // 2
    tm, tn = _tile(m, tm_pref, 8), _tile(f, tn_pref, LANES)
    chunk = _tile(tm, chunk_pref, 8)
    nj = f // tn
    lo = lambda rows: pl.BlockSpec((None, rows, tn), lambda i, j: (li, 0, j))
    hi = lambda rows: pl.BlockSpec((None, rows, tn), lambda i, j: (li, 0, j + nj))
    vmem = 2 * tm * d * 2 + 4 * d * tn * 2 + 2 * tm * tn * 2 + 2 * nj * CONV_HALO * tn * 4 + 6 * tm * tn * 4
    return pl.pallas_call(
        functools.partial(_ffn_in_kernel, chunk=chunk),
        grid=(m // tm, nj),
        in_specs=[pl.BlockSpec((tm, d), lambda i, j: (i, 0)),
                  lo(d), hi(d), lo(CONV_WIDTH), hi(CONV_WIDTH), lo(1), hi(1)],
        out_specs=pl.BlockSpec((tm, tn), lambda i, j: (i, j)),
        out_shape=jax.ShapeDtypeStruct((m, f), BF16),
        scratch_shapes=[pltpu.VMEM((nj, CONV_HALO, tn), F32),
                        pltpu.VMEM((nj, CONV_HALO, tn), F32)],
        compiler_params=_params(("arbitrary", "arbitrary"), vmem),
        name="ffn_in",
    )(h, w_in, w_in, conv_w, conv_w, _row(conv_b), _row(conv_b))


def _final_norm_kernel(x_ref, g_ref, o_ref):
    o_ref[...] = _rms(x_ref[...], g_ref[...])


def _final_norm(x, g, *, tm_pref=512):
    s, d = x.shape
    tm = _tile(s, tm_pref, 8)
    return pl.pallas_call(
        _final_norm_kernel,
        grid=(s // tm,),
        in_specs=[pl.BlockSpec((tm, d), lambda i: (i, 0)), pl.BlockSpec((1, d), lambda i: (0, 0))],
        out_specs=pl.BlockSpec((tm, d), lambda i: (i, 0)),
        out_shape=jax.ShapeDtypeStruct((s, d), F32),
        compiler_params=_params(("arbitrary",), 6 * tm * d * 4),
        name="final_norm",
    )(x, g.reshape(1, d))


def kernel(x, mem, positions, swa_norm, swa_w_qkv, swa_sinks, swa_w_o, diff_norm, diff_w_qkv, diff_lambda_q1, diff_lambda_k1, diff_lambda_q2, diff_lambda_k2, diff_subln, diff_w_o, pool_norm, pool_w, pool_scale, xattn_norm, xattn_mem_norm, xattn_w_q, xattn_w_kv, xattn_w_o, ffn_norm, ffn_w_in, ffn_conv_w, ffn_conv_b, ffn_w_out, final_norm):
    batch, seq, d = x.shape
    assert batch == 1, "the kernels fold no batch dimension"
    depth = ffn_w_in.shape[0]
    xs = x.reshape(seq, d)
    mem2 = mem.reshape(mem.shape[1], d)
    tables = _rope_tables(positions)
    bf = lambda w: w.astype(BF16)
    swa_w_qkv, swa_w_o, diff_w_qkv, diff_w_o, pool_w = map(bf, (swa_w_qkv, swa_w_o, diff_w_qkv, diff_w_o, pool_w))
    xattn_w_q, xattn_w_kv, xattn_w_o, ffn_w_in, ffn_w_out = map(
        bf, (xattn_w_q, xattn_w_kv, xattn_w_o, ffn_w_in, ffn_w_out))

    swa_hq = d // HEAD_DIM
    swa_hkv = (swa_w_qkv.shape[-1] // HEAD_DIM - swa_hq) // 2
    diff_heads = d // (2 * HEAD_DIM)
    diff_hkv = (diff_w_qkv.shape[-1] // (2 * HEAD_DIM) - diff_heads) // 2

    for i in range(depth):
        kind, j = i % N_MIXERS, i // N_MIXERS
        if kind == 0:
            qkv = _qkv_rope(xs, swa_norm, swa_w_qkv, j, tables, n_q=swa_hq * HEAD_DIM, n_k=swa_hkv * HEAD_DIM,
                            q_variant=ROPE_Q_SWA)
            att = _swa_attention(qkv, swa_sinks[j], hq=swa_hq, hkv=swa_hkv)
            xs = _matmul_residual(att, swa_w_o, j, xs, tm_pref=1024, tn_pref=512)
        elif kind == 1:
            lambda_init = 0.8 - 0.6 * math.exp(-0.3 * i)
            qkv = _qkv_rope(xs, diff_norm, diff_w_qkv, j, tables, n_q=2 * diff_heads * HEAD_DIM,
                            n_k=2 * diff_hkv * HEAD_DIM, q_variant=ROPE_Q_DIFF)
            lam_vecs = jnp.stack([diff_lambda_q1[j], diff_lambda_k1[j], diff_lambda_q2[j], diff_lambda_k2[j]])
            att = _diff_attention(qkv, lam_vecs, diff_subln, j, heads=diff_heads, hkv=diff_hkv,
                                  lambda_init=lambda_init)
            xs = _matmul_residual(att, diff_w_o, j, xs, tm_pref=1024, tn_pref=512)
        else:
            xs = _pool_mixer(xs, pool_norm, pool_w, pool_scale, j)
        kv = _norm_matmul(mem2, xattn_mem_norm, xattn_w_kv, i)
        xs, hn = _xattn(xs, xattn_norm, xattn_w_q, kv, xattn_w_o, ffn_norm, i)
        act = _ffn_in(hn, ffn_w_in, ffn_conv_w, ffn_conv_b, i)
        xs = _matmul_residual(act, ffn_w_out, i, xs, tm_pref=512, tn_pref=512)
    return _final_norm(xs, final_norm).reshape(batch, seq, d)
```

```python
import functools
import math

import jax
import jax.numpy as jnp
from jax import lax
from jax.experimental import pallas as pl
from jax.experimental.pallas import tpu as pltpu

HEAD_DIM = 128
ROPE_DIM = HEAD_DIM // 4
ROPE_HALF = ROPE_DIM // 2
ROPE_THETA = 500000.0
SWA_BLOCK = 128
POOL_WINDOWS = (2, 4, 8, 16)
POOL_HALO = 16
X_HEADS = 4
CONV_WIDTH = 3
CONV_HALO = 8
N_MIXERS = 3
NORM_EPS = 1e-5
LANES = 128
LOG2E = math.log2(math.e)

V7X_VMEM_BYTES = 64 * 2**20
VMEM_RESERVE_BYTES = 6 * 2**20

F32 = jnp.float32
BF16 = jnp.bfloat16

ROPE_Q_SWA, ROPE_Q_DIFF, ROPE_K, ROPE_NONE = 0, 1, 2, 3
ROPE_SCALES = (HEAD_DIM ** -0.5, HEAD_DIM ** -0.5 * LOG2E, 1.0)


def _params(semantics, vmem_bytes, flags=None):
    limit = min(int(vmem_bytes) * 5 // 4 + VMEM_RESERVE_BYTES, V7X_VMEM_BYTES - VMEM_RESERVE_BYTES // 2)
    return pltpu.CompilerParams(dimension_semantics=semantics, vmem_limit_bytes=limit, flags=flags)


def _tile(dim, pref, align):
    if dim <= pref:
        return dim
    t = (pref // align) * align
    while t > align and dim % t:
        t -= align
    assert dim % t == 0, (dim, pref, align)
    return t


def _rms(x, g, eps=NORM_EPS):
    ms = jnp.mean(x * x, axis=-1, keepdims=True)
    return x * lax.rsqrt(ms + eps) * g


def _row(stack):
    return stack.reshape(stack.shape[0], 1, stack.shape[-1])


def _rope_table_kernel(pos_ref, invf_ref, cos_ref, sa_ref, sb_ref):
    ang = pos_ref[...].astype(F32) * invf_ref[...]
    lane = lax.broadcasted_iota(jnp.int32, ang.shape, 1)
    s = jnp.sin(ang)
    cos = jnp.cos(ang)
    sa = jnp.where(lane >= ROPE_HALF, s, 0.0)
    sb = jnp.where(lane < ROPE_HALF, -s, 0.0)
    for v, scale in enumerate(ROPE_SCALES):
        cos_ref[v], sa_ref[v], sb_ref[v] = cos * scale, sa * scale, sb * scale
    cos_ref[ROPE_NONE] = jnp.ones(cos.shape, F32)
    sa_ref[ROPE_NONE] = jnp.zeros(cos.shape, F32)
    sb_ref[ROPE_NONE] = jnp.zeros(cos.shape, F32)


def _rope_tables(positions):
    s = positions.shape[-1]
    inv_freq = ROPE_THETA ** (-jnp.arange(0, ROPE_DIM, 2, dtype=F32) / ROPE_DIM)
    invf = jnp.zeros((1, LANES), F32).at[0, :ROPE_DIM].set(jnp.concatenate([inv_freq, inv_freq]))
    nv = len(ROPE_SCALES) + 1
    tm = _tile(s, 1024, 8)
    tab = jax.ShapeDtypeStruct((nv, s, LANES), F32)
    return pl.pallas_call(
        _rope_table_kernel,
        grid=(s // tm,),
        in_specs=[pl.BlockSpec((tm, 1), lambda i: (i, 0)), pl.BlockSpec((1, LANES), lambda i: (0, 0))],
        out_specs=[pl.BlockSpec((nv, tm, LANES), lambda i: (0, i, 0))] * 3,
        out_shape=[tab] * 3,
        compiler_params=_params(("arbitrary",), (6 * nv + 8) * tm * LANES * 4),
        name="rope_tables",
    )(positions.reshape(s, 1), invf)


def _norm_to_scratch(x_ref, g_ref, h_ref):
    @pl.when(pl.program_id(1) == 0)
    def _():
        h_ref[...] = _rms(x_ref[...], g_ref[...]).astype(h_ref.dtype)


def _norm_matmul_kernel(x_ref, g_ref, w_ref, o_ref, h_ref):
    _norm_to_scratch(x_ref, g_ref, h_ref)
    o_ref[...] = jnp.dot(h_ref[...], w_ref[...], preferred_element_type=F32).astype(o_ref.dtype)


def _norm_matmul(x, g, w, li, *, tm_pref=512, tn_pref=1024):
    m, d = x.shape
    n = w.shape[-1]
    tm, tn = _tile(m, tm_pref, 8), _tile(n, tn_pref, LANES)
    vmem = 2 * tm * d * 4 + tm * d * 2 + 2 * d * tn * 2 + 2 * tm * tn * 2 + 2 * tm * tn * 4
    return pl.pallas_call(
        _norm_matmul_kernel,
        grid=(m // tm, n // tn),
        in_specs=[pl.BlockSpec((tm, d), lambda i, j: (i, 0)),
                  pl.BlockSpec((None, 1, d), lambda i, j: (li, 0, 0)),
                  pl.BlockSpec((None, d, tn), lambda i, j: (li, 0, j))],
        out_specs=pl.BlockSpec((tm, tn), lambda i, j: (i, j)),
        out_shape=jax.ShapeDtypeStruct((m, n), BF16),
        scratch_shapes=[pltpu.VMEM((tm, d), BF16)],
        compiler_params=_params(("arbitrary", "arbitrary"), vmem),
        name="norm_matmul",
    )(x, _row(g), w)


def _qkv_rope_kernel(x_ref, g_ref, w_ref, cos_ref, sa_ref, sb_ref, o_ref, h_ref):
    _norm_to_scratch(x_ref, g_ref, h_ref)
    y = jnp.dot(h_ref[...], w_ref[...], preferred_element_type=F32)
    cos, sa, sb = cos_ref[...], sa_ref[...], sb_ref[...]
    for c in range(y.shape[1] // HEAD_DIM):
        sl = slice(c * HEAD_DIM, (c + 1) * HEAD_DIM)
        t = y[:, sl]
        r = t * cos + pltpu.roll(t, ROPE_HALF, 1) * sa + pltpu.roll(t, HEAD_DIM - ROPE_HALF, 1) * sb
        o_ref[:, sl] = r.astype(o_ref.dtype)


def _qkv_rope(x, g, w, li, tables, *, n_q, n_k, q_variant, tm_pref=512):
    m, d = x.shape
    n = w.shape[-1]
    tn = _tile(math.gcd(n_q, n_k, n - n_q - n_k), 1024, HEAD_DIM)
    tm = _tile(m, tm_pref, 8)
    nqb, nkb = n_q // tn, n_k // tn

    def tab_index(i, j):
        return (jnp.where(j < nqb, q_variant, jnp.where(j < nqb + nkb, ROPE_K, ROPE_NONE)), i, 0)

    tab_spec = pl.BlockSpec((None, tm, LANES), tab_index)
    vmem = 2 * tm * d * 4 + tm * d * 2 + 2 * d * tn * 2 + 2 * tm * tn * 2 + 3 * tm * tn * 4
    return pl.pallas_call(
        _qkv_rope_kernel,
        grid=(m // tm, n // tn),
        in_specs=[pl.BlockSpec((tm, d), lambda i, j: (i, 0)),
                  pl.BlockSpec((None, 1, d), lambda i, j: (li, 0, 0)),
                  pl.BlockSpec((None, d, tn), lambda i, j: (li, 0, j)),
                  tab_spec, tab_spec, tab_spec],
        out_specs=pl.BlockSpec((tm, tn), lambda i, j: (i, j)),
        out_shape=jax.ShapeDtypeStruct((m, n), BF16),
        scratch_shapes=[pltpu.VMEM((tm, d), BF16)],
        compiler_params=_params(("arbitrary", "arbitrary"), vmem),
        name="qkv_rope",
    )(x, _row(g), w, *tables)


def _matmul_residual_kernel(a_ref, w_ref, x_ref, o_ref):
    o_ref[...] = x_ref[...] + jnp.dot(a_ref[...], w_ref[...], preferred_element_type=F32)


def _matmul_residual(a, w, li, x, *, tm_pref, tn_pref):
    m, kd = a.shape
    n = w.shape[-1]
    tm, tn = _tile(m, tm_pref, 8), _tile(n, tn_pref, LANES)
    vmem = 2 * tm * kd * 2 + 2 * kd * tn * 2 + 5 * tm * tn * 4
    return pl.pallas_call(
        _matmul_residual_kernel,
        grid=(m // tm, n // tn),
        in_specs=[pl.BlockSpec((tm, kd), lambda i, j: (i, 0)),
                  pl.BlockSpec((None, kd, tn), lambda i, j: (li, 0, j)),
                  pl.BlockSpec((tm, tn), lambda i, j: (i, j))],
        out_specs=pl.BlockSpec((tm, tn), lambda i, j: (i, j)),
        out_shape=jax.ShapeDtypeStruct((m, n), F32),
        compiler_params=_params(("arbitrary", "arbitrary"), vmem),
        name="matmul_residual",
    )(a, w, x)


def _swa_kernel(sink_ref, q_ref, kc_ref, kp_ref, vc_ref, vp_ref, o_ref, *, group, hps):
    i, hb = pl.program_id(0), pl.program_id(1)
    blk = SWA_BLOCK
    nsub = q_ref.shape[0] // blk
    row = lax.broadcasted_iota(jnp.int32, (blk, 2 * blk), 0)
    col = lax.broadcasted_iota(jnp.int32, (blk, 2 * blk), 1)
    rel = col - blk
    band = (rel <= row) & (row - rel < blk)
    nt = (((1,), (1,)), ((), ()))
    for c in range(nsub):
        rows = slice(c * blk, (c + 1) * blk)
        if c == 0:
            mask = band & (rel >= jnp.where(i > 0, -blk, 0))
        else:
            prev = slice((c - 1) * blk, c * blk)
            mask = band
        for hh in range(hps):
            kcols = slice(hh * HEAD_DIM, (hh + 1) * HEAD_DIM)
            if c == 0:
                kprev, vprev = kp_ref[:, kcols], vp_ref[:, kcols]
            else:
                kprev, vprev = kc_ref[prev, kcols], vc_ref[prev, kcols]
            kw = jnp.concatenate([kprev, kc_ref[rows, kcols]], axis=0)
            vw = jnp.concatenate([vprev, vc_ref[rows, kcols]], axis=0)
            qcol = lambda g: slice((hh * group + g) * HEAD_DIM, (hh * group + g + 1) * HEAD_DIM)
            qs = jnp.concatenate([q_ref[rows, qcol(g)] for g in range(group)], axis=0)
            s = lax.dot_general(qs, kw, nt, preferred_element_type=F32)
            probs, inv = [], []
            for g in range(group):
                sg = jnp.where(mask, s[g * blk:(g + 1) * blk], -jnp.inf)
                sink = sink_ref[(hb * hps + hh) * group + g]
                mx = jnp.maximum(jnp.max(sg, axis=-1, keepdims=True), sink)
                p = jnp.exp(sg - mx)
                inv.append(1.0 / (jnp.sum(p, axis=-1, keepdims=True) + jnp.exp(sink - mx)))
                probs.append(p.astype(BF16))
            o = jnp.dot(jnp.concatenate(probs, axis=0), vw, preferred_element_type=F32)
            for g in range(group):
                o_ref[rows, qcol(g)] = (o[g * blk:(g + 1) * blk] * inv[g]).astype(o_ref.dtype)


def _swa_attention(qkv, sinks, *, hq, hkv, tq_pref=512):
    s = qkv.shape[0]
    group = hq // hkv
    hps = 2 if hkv % 2 == 0 else 1
    tq = _tile(s, tq_pref, SWA_BLOCK)
    r = tq // SWA_BLOCK
    kw = hps * HEAD_DIM
    cur = lambda off: pl.BlockSpec((tq, kw), lambda i, h, sk: (i, off // hps + h))
    prev = lambda off: pl.BlockSpec((SWA_BLOCK, kw), lambda i, h, sk: (jnp.maximum(i * r - 1, 0), off // hps + h))
    vmem = (4 * tq * hps * group * HEAD_DIM * 2 + 8 * tq * kw * 2
            + 8 * hps * group * SWA_BLOCK * 2 * SWA_BLOCK * 4)
    return pl.pallas_call(
        functools.partial(_swa_kernel, group=group, hps=hps),
        grid_spec=pltpu.PrefetchScalarGridSpec(
            num_scalar_prefetch=1,
            grid=(s // tq, hkv // hps),
            in_specs=[pl.BlockSpec((tq, hps * group * HEAD_DIM), lambda i, h, sk: (i, h)),
                      cur(hq), prev(hq), cur(hq + hkv), prev(hq + hkv)],
            out_specs=pl.BlockSpec((tq, hps * group * HEAD_DIM), lambda i, h, sk: (i, h)),
        ),
        out_shape=jax.ShapeDtypeStruct((s, hq * HEAD_DIM), BF16),
        compiler_params=_params(("arbitrary", "arbitrary"), vmem),
        name="swa_attention",
    )(sinks, qkv, qkv, qkv, qkv, qkv)


def _diff_kernel(lam_ref, subln_ref, q_ref, k_ref, v_ref, o_ref, qs_ref, m_ref, l_ref, acc_ref, sa_ref, sb_ref,
                 *, group, tk, lambda_init):
    qi = pl.program_id(1)
    tq = q_ref.shape[0]
    rows = group * tq
    nt = (((1,), (1,)), ((), ()))

    for mp in range(2):
        for g in range(group):
            c = (g * 2 + mp) * HEAD_DIM
            qs_ref[mp, g * tq:(g + 1) * tq, :] = q_ref[:, c:c + HEAD_DIM]
    m_ref[...] = jnp.full(m_ref.shape, -jnp.inf, F32)
    l_ref[...] = jnp.zeros(l_ref.shape, F32)
    acc_ref[...] = jnp.zeros(acc_ref.shape, F32)

    def scores(kb, dst_ref):
        st = pl.multiple_of(kb * tk, tk)
        for mp in range(2):
            dst_ref[mp] = lax.dot_general(qs_ref[mp], k_ref[pl.ds(st, tk), mp * HEAD_DIM:(mp + 1) * HEAD_DIM], nt,
                                          preferred_element_type=F32)

    def update(kb, src_ref, masked):
        st = pl.multiple_of(kb * tk, tk)
        vblk = v_ref[pl.ds(st, tk), :]
        if masked:
            qpos = qi * tq + (lax.broadcasted_iota(jnp.int32, (rows, tk), 0) & (tq - 1))
            kpos = st + lax.broadcasted_iota(jnp.int32, (rows, tk), 1)
            visible = kpos <= qpos
        for mp in range(2):
            s = src_ref[mp]
            if masked:
                s = jnp.where(visible, s, -jnp.inf)
            m_prev = m_ref[mp]
            m_next = jnp.maximum(m_prev, jnp.max(s, axis=-1, keepdims=True))
            alpha = jnp.exp2(m_prev - m_next)
            p = jnp.exp2(s - pltpu.repeat(m_next, tk // LANES, 1))
            lane_sum = p[:, :LANES]
            for t in range(1, tk // LANES):
                lane_sum = lane_sum + p[:, t * LANES:(t + 1) * LANES]
            l_ref[mp] = alpha * l_ref[mp] + lane_sum
            pv = jnp.dot(p.astype(BF16), vblk, preferred_element_type=F32)
            acc_ref[mp] = acc_ref[mp] * pltpu.repeat(alpha, acc_ref.shape[2] // LANES, 1) + pv
            m_ref[mp] = m_next

    n_full = (qi * tq) // tk
    scores(0, sa_ref)

    def body(t, carry):
        @pl.when(t % 2 == 0)
        def _():
            scores(t + 1, sb_ref)
            update(t, sa_ref, False)

        @pl.when(t % 2 == 1)
        def _():
            scores(t + 1, sa_ref)
            update(t, sb_ref, False)

        return carry

    lax.fori_loop(0, n_full, body, 0)

    @pl.when(n_full % 2 == 0)
    def _():
        update(n_full, sa_ref, True)

    @pl.when(n_full % 2 == 1)
    def _():
        update(n_full, sb_ref, True)

    lam_vecs = lam_ref[...]
    lam = (jnp.exp(jnp.sum(lam_vecs[0:1] * lam_vecs[1:2], axis=-1, keepdims=True))
           - jnp.exp(jnp.sum(lam_vecs[2:3] * lam_vecs[3:4], axis=-1, keepdims=True)) + lambda_init)
    o = (acc_ref[0] / jnp.sum(l_ref[0], axis=-1, keepdims=True)
         - lam * (acc_ref[1] / jnp.sum(l_ref[1], axis=-1, keepdims=True)))
    y = _rms(o, subln_ref[...]) * (1.0 - lambda_init)
    vd = y.shape[1]
    for g in range(group):
        o_ref[:, g * vd:(g + 1) * vd] = y[g * tq:(g + 1) * tq].astype(o_ref.dtype)


def _diff_attention(qkv, lam_vecs, subln, li, *, heads, hkv, lambda_init, tq_pref=256, tk_pref=1024):
    s = qkv.shape[0]
    group = heads // hkv
    vd = 2 * HEAD_DIM
    tq = _tile(s, tq_pref, 8)
    tk = _tile(s, tk_pref, LANES)
    assert tk % tq == 0 and tq & (tq - 1) == 0, (tq, tk)
    rows = group * tq
    vmem = (2 * s * vd * 2 + 4 * tq * group * vd * 2 + 2 * rows * HEAD_DIM * 2
            + 4 * rows * LANES * 4 + 2 * rows * vd * 4 + 4 * rows * tk * 4 + 2 * rows * tk * 4)
    kern = functools.partial(_diff_kernel, group=group, tk=tk, lambda_init=lambda_init)
    resident = lambda off: pl.BlockSpec((s, vd), lambda h, i: (0, off + h), pipeline_mode=pl.Buffered(1))
    return pl.pallas_call(
        kern,
        grid=(hkv, s // tq),
        in_specs=[pl.BlockSpec((4, HEAD_DIM), lambda h, i: (0, 0)),
                  pl.BlockSpec((None, 1, vd), lambda h, i: (li, 0, 0)),
                  pl.BlockSpec((tq, group * vd), lambda h, i: (i, h)),
                  resident(heads), resident(heads + hkv)],
        out_specs=pl.BlockSpec((tq, group * vd), lambda h, i: (i, h)),
        out_shape=jax.ShapeDtypeStruct((s, heads * vd), BF16),
        scratch_shapes=[pltpu.VMEM((2, rows, HEAD_DIM), BF16),
                        pltpu.VMEM((2, rows, LANES), F32),
                        pltpu.VMEM((2, rows, LANES), F32),
                        pltpu.VMEM((2, rows, vd), F32),
                        pltpu.VMEM((2, rows, tk), F32),
                        pltpu.VMEM((2, rows, tk), F32)],
        compiler_params=_params(("arbitrary", "arbitrary"), vmem),
        name="diff_attention",
    )(lam_vecs, _row(subln), qkv, qkv, qkv)


def _pool_kernel(x_ref, xp_ref, g_ref, w_ref, scale_ref, o_ref, ext_ref):
    i = pl.program_id(0)
    tm = x_ref.shape[0]
    ch = w_ref.shape[1]
    x = x_ref[...]
    h = _rms(x, g_ref[...])
    hp = _rms(xp_ref[...], g_ref[...])
    ext_ref[0:POOL_HALO, :] = jnp.where(i > 0, hp, 0.0)
    ext_ref[POOL_HALO:, :] = h
    t = i * tm + lax.broadcasted_iota(jnp.int32, (tm, 1), 0)
    for gi, win in enumerate(POOL_WINDOWS):
        cols = slice(gi * ch, (gi + 1) * ch)
        e = ext_ref[:, cols]
        span = 1
        while span < win:
            e = e + pltpu.roll(e, span, 0)
            span *= 2
        count = jnp.minimum(t + 1, win).astype(F32)
        pooled = (e[POOL_HALO:] / count - h[:, cols]).astype(BF16)
        y = jnp.dot(pooled, w_ref[gi], preferred_element_type=F32)
        o_ref[:, cols] = x[:, cols] + y * scale_ref[:, cols]


def _pool_mixer(x, g, w, scale, li, *, tm_pref=256):
    s, d = x.shape
    groups, ch = w.shape[1], w.shape[2]
    assert groups == len(POOL_WINDOWS)
    tm = _tile(s, tm_pref, POOL_HALO)
    r = tm // POOL_HALO
    vec = pl.BlockSpec((None, 1, d), lambda i: (li, 0, 0))
    vmem = 4 * tm * d * 4 + 2 * groups * ch * ch * 2 + (tm + POOL_HALO) * d * 4 + 6 * tm * ch * 4
    return pl.pallas_call(
        _pool_kernel,
        grid=(s // tm,),
        in_specs=[pl.BlockSpec((tm, d), lambda i: (i, 0)),
                  pl.BlockSpec((POOL_HALO, d), lambda i: (jnp.maximum(i * r - 1, 0), 0)),
                  vec,
                  pl.BlockSpec((None, groups, ch, ch), lambda i: (li, 0, 0, 0)),
                  vec],
        out_specs=pl.BlockSpec((tm, d), lambda i: (i, 0)),
        out_shape=jax.ShapeDtypeStruct((s, d), F32),
        scratch_shapes=[pltpu.VMEM((tm + POOL_HALO, d), F32)],
        compiler_params=_params(("arbitrary",), vmem),
        name="pool_mixer",
    )(x, x, _row(g), w, _row(scale))


def _xattn_kernel(x_ref, g_ref, wq_ref, kv_ref, wo_ref, gn_ref, o_ref, hn_ref, *, heads):
    x = x_ref[...]
    h = _rms(x, g_ref[...]).astype(BF16)
    width = wq_ref.shape[1]
    hd = width // heads
    q = (jnp.dot(h, wq_ref[...], preferred_element_type=F32) * (hd ** -0.5)).astype(BF16)
    nt = (((1,), (1,)), ((), ()))
    outs = []
    for a in range(heads):
        cols = slice(a * hd, (a + 1) * hd)
        s = lax.dot_general(q[:, cols], kv_ref[:, cols], nt, preferred_element_type=F32)
        p = jnp.exp(s - jnp.max(s, axis=-1, keepdims=True))
        den = jnp.sum(p, axis=-1, keepdims=True)
        pv = jnp.dot(p.astype(BF16), kv_ref[:, width + a * hd:width + (a + 1) * hd], preferred_element_type=F32)
        outs.append((pv / den).astype(BF16))
    o = jnp.concatenate(outs, axis=-1)
    y = x + jnp.dot(o, wo_ref[...], preferred_element_type=F32)
    o_ref[...] = y
    hn_ref[...] = _rms(y, gn_ref[...]).astype(hn_ref.dtype)


def _xattn(x, g, wq, kv, wo, g_next, li, *, tm_pref=256):
    s, d = x.shape
    width = wq.shape[-1]
    mem_len = kv.shape[0]
    tm = _tile(s, tm_pref, 8)
    vmem = (4 * tm * d * 4 + 2 * tm * d * 2 + 2 * d * width * 2 + mem_len * 2 * width * 2 + tm * d * 6
            + 4 * tm * width * 4)
    layer = lambda shape: pl.BlockSpec((None,) + shape, lambda i: (li, 0, 0), pipeline_mode=pl.Buffered(1))
    rows = pl.BlockSpec((tm, d), lambda i: (i, 0))
    return pl.pallas_call(
        functools.partial(_xattn_kernel, heads=X_HEADS),
        grid=(s // tm,),
        in_specs=[rows, layer((1, d)), layer((d, width)),
                  pl.BlockSpec((mem_len, 2 * width), lambda i: (0, 0), pipeline_mode=pl.Buffered(1)),
                  layer((width, d)), layer((1, d))],
        out_specs=[rows, rows],
        out_shape=[jax.ShapeDtypeStruct((s, d), F32), jax.ShapeDtypeStruct((s, d), BF16)],
        compiler_params=_params(("arbitrary",), vmem),
        name="xattn",
    )(x, _row(g), wq, kv, wo, _row(g_next))


def _ffn_in_kernel(h_ref, wg_ref, wv_ref, cwg_ref, cwv_ref, cbg_ref, cbv_ref, o_ref, cg_ref, cv_ref, *, chunk):
    i, j = pl.program_id(0), pl.program_id(1)
    tm = h_ref.shape[0]

    @pl.when(i == 0)
    def _():
        cg_ref[j] = jnp.zeros(cg_ref.shape[1:], F32)
        cv_ref[j] = jnp.zeros(cv_ref.shape[1:], F32)

    def conv(w_ref, cw_ref, cb_ref, carry_ref):
        u = jnp.dot(h_ref[...], w_ref[...], preferred_element_type=F32)
        first = carry_ref[j]
        carry_ref[j] = u[tm - CONV_HALO:, :]
        out = []
        for r0 in range(0, tm, chunk):
            prev = first if r0 == 0 else u[r0 - CONV_HALO:r0, :]
            ext = jnp.concatenate([prev, u[r0:r0 + chunk, :]], axis=0)
            c = cb_ref[...]
            for tap in range(CONV_WIDTH):
                lag = CONV_WIDTH - 1 - tap
                shifted = pltpu.roll(ext, lag, 0) if lag else ext
                c = c + cw_ref[tap:tap + 1, :] * shifted[CONV_HALO:, :]
            out.append(c)
        return out

    gate = conv(wg_ref, cwg_ref, cbg_ref, cg_ref)
    val = conv(wv_ref, cwv_ref, cbv_ref, cv_ref)
    for k, (gk, vk) in enumerate(zip(gate, val)):
        o_ref[k * chunk:(k + 1) * chunk, :] = (gk / (1.0 + jnp.exp(-gk)) * vk).astype(o_ref.dtype)


def _ffn_in(h, w_in, conv_w, conv_b, li, *, tm_pref=1024, tn_pref=512, chunk_pref=256):
    m, d = h.shape
    f = w_in.shape[-1] // 2
    tm, tn = _tile(m, tm_pref, 8), _tile(f, tn_pref, LANES)
    chunk = _tile(tm, chunk_pref, 8)
    nj = f // tn
    lo = lambda rows: pl.BlockSpec((None, rows, tn), lambda i, j: (li, 0, j))
    hi = lambda rows: pl.BlockSpec((None, rows, tn), lambda i, j: (li, 0, j + nj))
    vmem = 2 * tm * d * 2 + 4 * d * tn * 2 + 2 * tm * tn * 2 + 2 * nj * CONV_HALO * tn * 4 + 6 * tm * tn * 4
    return pl.pallas_call(
        functools.partial(_ffn_in_kernel, chunk=chunk),
        grid=(m // tm, nj),
        in_specs=[pl.BlockSpec((tm, d), lambda i, j: (i, 0)),
                  lo(d), hi(d), lo(CONV_WIDTH), hi(CONV_WIDTH), lo(1), hi(1)],
        out_specs=pl.BlockSpec((tm, tn), lambda i, j: (i, j)),
        out_shape=jax.ShapeDtypeStruct((m, f), BF16),
        scratch_shapes=[pltpu.VMEM((nj, CONV_HALO, tn), F32),
                        pltpu.VMEM((nj, CONV_HALO, tn), F32)],
        compiler_params=_params(("arbitrary", "arbitrary"), vmem),
        name="ffn_in",
    )(h, w_in, w_in, conv_w, conv_w, _row(conv_b), _row(conv_b))


def _final_norm_kernel(x_ref, g_ref, o_ref):
    o_ref[...] = _rms(x_ref[...], g_ref[...])


def _final_norm(x, g, *, tm_pref=512):
    s, d = x.shape
    tm = _tile(s, tm_pref, 8)
    return pl.pallas_call(
        _final_norm_kernel,
        grid=(s // tm,),
        in_specs=[pl.BlockSpec((tm, d), lambda i: (i, 0)), pl.BlockSpec((1, d), lambda i: (0, 0))],
        out_specs=pl.BlockSpec((tm, d), lambda i: (i, 0)),
        out_shape=jax.ShapeDtypeStruct((s, d), F32),
        compiler_params=_params(("arbitrary",), 6 * tm * d * 4),
        name="final_norm",
    )(x, g.reshape(1, d))


def kernel(x, mem, positions, swa_norm, swa_w_qkv, swa_sinks, swa_w_o, diff_norm, diff_w_qkv, diff_lambda_q1, diff_lambda_k1, diff_lambda_q2, diff_lambda_k2, diff_subln, diff_w_o, pool_norm, pool_w, pool_scale, xattn_norm, xattn_mem_norm, xattn_w_q, xattn_w_kv, xattn_w_o, ffn_norm, ffn_w_in, ffn_conv_w, ffn_conv_b, ffn_w_out, final_norm):
    batch, seq, d = x.shape
    assert batch == 1, "the kernels fold no batch dimension"
    depth = ffn_w_in.shape[0]
    xs = x.reshape(seq, d)
    mem2 = mem.reshape(mem.shape[1], d)
    tables = _rope_tables(positions)
    bf = lambda w: w.astype(BF16)
    swa_w_qkv, swa_w_o, diff_w_qkv, diff_w_o, pool_w = map(bf, (swa_w_qkv, swa_w_o, diff_w_qkv, diff_w_o, pool_w))
    xattn_w_q, xattn_w_kv, xattn_w_o, ffn_w_in, ffn_w_out = map(
        bf, (xattn_w_q, xattn_w_kv, xattn_w_o, ffn_w_in, ffn_w_out))

    swa_hq = d // HEAD_DIM
    swa_hkv = (swa_w_qkv.shape[-1] // HEAD_DIM - swa_hq) // 2
    diff_heads = d // (2 * HEAD_DIM)
    diff_hkv = (diff_w_qkv.shape[-1] // (2 * HEAD_DIM) - diff_heads) // 2

    for i in range(depth):
        kind, j = i % N_MIXERS, i // N_MIXERS
        if kind == 0:
            qkv = _qkv_rope(xs, swa_norm, swa_w_qkv, j, tables, n_q=swa_hq * HEAD_DIM, n_k=swa_hkv * HEAD_DIM,
                            q_variant=ROPE_Q_SWA)
            att = _swa_attention(qkv, swa_sinks[j], hq=swa_hq, hkv=swa_hkv)
            xs = _matmul_residual(att, swa_w_o, j, xs, tm_pref=1024, tn_pref=512)
        elif kind == 1:
            lambda_init = 0.8 - 0.6 * math.exp(-0.3 * i)
            qkv = _qkv_rope(xs, diff_norm, diff_w_qkv, j, tables, n_q=2 * diff_heads * HEAD_DIM,
                            n_k=2 * diff_hkv * HEAD_DIM, q_variant=ROPE_Q_DIFF)
            lam_vecs = jnp.stack([diff_lambda_q1[j], diff_lambda_k1[j], diff_lambda_q2[j], diff_lambda_k2[j]])
            att = _diff_attention(qkv, lam_vecs, diff_subln, j, heads=diff_heads, hkv=diff_hkv,
                                  lambda_init=lambda_init)
            xs = _matmul_residual(att, diff_w_o, j, xs, tm_pref=1024, tn_pref=512)
        else:
            xs = _pool_mixer(xs, pool_norm, pool_w, pool_scale, j)
        kv = _norm_matmul(mem2, xattn_mem_norm, xattn_w_kv, i)
        xs, hn = _xattn(xs, xattn_norm, xattn_w_q, kv, xattn_w_o, ffn_norm, i)
        act = _ffn_in(hn, ffn_w_in, ffn_conv_w, ffn_conv_b, i)
        xs = _matmul_residual(act, ffn_w_out, i, xs, tm_pref=512, tn_pref=512)
    return _final_norm(xs, final_norm).reshape(batch, seq, d)
```

```python
import functools
import math

import jax
import jax.numpy as jnp
from jax import lax
from jax.experimental import pallas as pl
from jax.experimental.pallas import tpu as pltpu

HEAD_DIM = 128
ROPE_DIM = HEAD_DIM // 4
ROPE_HALF = ROPE_DIM // 2
ROPE_THETA = 500000.0
SWA_BLOCK = 128
POOL_WINDOWS = (2, 4, 8, 16)
POOL_HALO = 16
X_HEADS = 4
CONV_WIDTH = 3
CONV_HALO = 8
N_MIXERS = 3
NORM_EPS = 1e-5
LANES = 128
LOG2E = math.log2(math.e)

V7X_VMEM_BYTES = 64 * 2**20
VMEM_RESERVE_BYTES = 6 * 2**20

F32 = jnp.float32
BF16 = jnp.bfloat16

ROPE_Q_SWA, ROPE_Q_DIFF, ROPE_K, ROPE_NONE = 0, 1, 2, 3
ROPE_SCALES = (HEAD_DIM ** -0.5, HEAD_DIM ** -0.5 * LOG2E, 1.0)


def _params(semantics, vmem_bytes, flags=None):
    limit = min(int(vmem_bytes) * 5 // 4 + VMEM_RESERVE_BYTES, V7X_VMEM_BYTES - VMEM_RESERVE_BYTES // 2)
    return pltpu.CompilerParams(dimension_semantics=semantics, vmem_limit_bytes=limit, flags=flags)


def _tile(dim, pref, align):
    if dim <= pref:
        return dim
    t = (pref // align) * align
    while t > align and dim % t:
        t -= align
    assert dim % t == 0, (dim, pref, align)
    return t


def _rms(x, g, eps=NORM_EPS):
    ms = jnp.mean(x * x, axis=-1, keepdims=True)
    return x * lax.rsqrt(ms + eps) * g


def _row(stack):
    return stack.reshape(stack.shape[0], 1, stack.shape[-1])


def _rope_table_kernel(pos_ref, invf_ref, cos_ref, sa_ref, sb_ref):
    ang = pos_ref[...].astype(F32) * invf_ref[...]
    lane = lax.broadcasted_iota(jnp.int32, ang.shape, 1)
    s = jnp.sin(ang)
    cos = jnp.cos(ang)
    sa = jnp.where(lane >= ROPE_HALF, s, 0.0)
    sb = jnp.where(lane < ROPE_HALF, -s, 0.0)
    for v, scale in enumerate(ROPE_SCALES):
        cos_ref[v], sa_ref[v], sb_ref[v] = cos * scale, sa * scale, sb * scale
    cos_ref[ROPE_NONE] = jnp.ones(cos.shape, F32)
    sa_ref[ROPE_NONE] = jnp.zeros(cos.shape, F32)
    sb_ref[ROPE_NONE] = jnp.zeros(cos.shape, F32)


def _rope_tables(positions):
    s = positions.shape[-1]
    inv_freq = ROPE_THETA ** (-jnp.arange(0, ROPE_DIM, 2, dtype=F32) / ROPE_DIM)
    invf = jnp.zeros((1, LANES), F32).at[0, :ROPE_DIM].set(jnp.concatenate([inv_freq, inv_freq]))
    nv = len(ROPE_SCALES) + 1
    tm = _tile(s, 1024, 8)
    tab = jax.ShapeDtypeStruct((nv, s, LANES), F32)
    return pl.pallas_call(
        _rope_table_kernel,
        grid=(s // tm,),
        in_specs=[pl.BlockSpec((tm, 1), lambda i: (i, 0)), pl.BlockSpec((1, LANES), lambda i: (0, 0))],
        out_specs=[pl.BlockSpec((nv, tm, LANES), lambda i: (0, i, 0))] * 3,
        out_shape=[tab] * 3,
        compiler_params=_params(("arbitrary",), (6 * nv + 8) * tm * LANES * 4),
        name="rope_tables",
    )(positions.reshape(s, 1), invf)


def _norm_to_scratch(x_ref, g_ref, h_ref):
    @pl.when(pl.program_id(1) == 0)
    def _():
        h_ref[...] = _rms(x_ref[...], g_ref[...]).astype(h_ref.dtype)


def _norm_matmul_kernel(x_ref, g_ref, w_ref, o_ref, h_ref):
    _norm_to_scratch(x_ref, g_ref, h_ref)
    o_ref[...] = jnp.dot(h_ref[...], w_ref[...], preferred_element_type=F32).astype(o_ref.dtype)


def _norm_matmul(x, g, w, li, *, tm_pref=512, tn_pref=1024):
    m, d = x.shape
    n = w.shape[-1]
    tm, tn = _tile(m, tm_pref, 8), _tile(n, tn_pref, LANES)
    vmem = 2 * tm * d * 4 + tm * d * 2 + 2 * d * tn * 2 + 2 * tm * tn * 2 + 2 * tm * tn * 4
    return pl.pallas_call(
        _norm_matmul_kernel,
        grid=(m // tm, n // tn),
        in_specs=[pl.BlockSpec((tm, d), lambda i, j: (i, 0)),
                  pl.BlockSpec((None, 1, d), lambda i, j: (li, 0, 0)),
                  pl.BlockSpec((None, d, tn), lambda i, j: (li, 0, j))],
        out_specs=pl.BlockSpec((tm, tn), lambda i, j: (i, j)),
        out_shape=jax.ShapeDtypeStruct((m, n), BF16),
        scratch_shapes=[pltpu.VMEM((tm, d), BF16)],
        compiler_params=_params(("arbitrary", "arbitrary"), vmem),
        name="norm_matmul",
    )(x, _row(g), w)


def _qkv_rope_kernel(x_ref, g_ref, w_ref, cos_ref, sa_ref, sb_ref, o_ref, h_ref):
    _norm_to_scratch(x_ref, g_ref, h_ref)
    y = jnp.dot(h_ref[...], w_ref[...], preferred_element_type=F32)
    cos, sa, sb = cos_ref[...], sa_ref[...], sb_ref[...]
    for c in range(y.shape[1] // HEAD_DIM):
        sl = slice(c * HEAD_DIM, (c + 1) * HEAD_DIM)
        t = y[:, sl]
        r = t * cos + pltpu.roll(t, ROPE_HALF, 1) * sa + pltpu.roll(t, HEAD_DIM - ROPE_HALF, 1) * sb
        o_ref[:, sl] = r.astype(o_ref.dtype)


def _qkv_rope(x, g, w, li, tables, *, n_q, n_k, q_variant, tm_pref=512):
    m, d = x.shape
    n = w.shape[-1]
    tn = _tile(math.gcd(n_q, n_k, n - n_q - n_k), 1024, HEAD_DIM)
    tm = _tile(m, tm_pref, 8)
    nqb, nkb = n_q // tn, n_k // tn

    def tab_index(i, j):
        return (jnp.where(j < nqb, q_variant, jnp.where(j < nqb + nkb, ROPE_K, ROPE_NONE)), i, 0)

    tab_spec = pl.BlockSpec((None, tm, LANES), tab_index)
    vmem = 2 * tm * d * 4 + tm * d * 2 + 2 * d * tn * 2 + 2 * tm * tn * 2 + 3 * tm * tn * 4
    return pl.pallas_call(
        _qkv_rope_kernel,
        grid=(m // tm, n // tn),
        in_specs=[pl.BlockSpec((tm, d), lambda i, j: (i, 0)),
                  pl.BlockSpec((None, 1, d), lambda i, j: (li, 0, 0)),
                  pl.BlockSpec((None, d, tn), lambda i, j: (li, 0, j)),
                  tab_spec, tab_spec, tab_spec],
        out_specs=pl.BlockSpec((tm, tn), lambda i, j: (i, j)),
        out_shape=jax.ShapeDtypeStruct((m, n), BF16),
        scratch_shapes=[pltpu.VMEM((tm, d), BF16)],
        compiler_params=_params(("arbitrary", "arbitrary"), vmem),
        name="qkv_rope",
    )(x, _row(g), w, *tables)


def _matmul_residual_kernel(a_ref, w_ref, x_ref, o_ref):
    o_ref[...] = x_ref[...] + jnp.dot(a_ref[...], w_ref[...], preferred_element_type=F32)


def _matmul_residual(a, w, li, x, *, tm_pref, tn_pref):
    m, kd = a.shape
    n = w.shape[-1]
    tm, tn = _tile(m, tm_pref, 8), _tile(n, tn_pref, LANES)
    vmem = 2 * tm * kd * 2 + 2 * kd * tn * 2 + 5 * tm * tn * 4
    return pl.pallas_call(
        _matmul_residual_kernel,
        grid=(m // tm, n // tn),
        in_specs=[pl.BlockSpec((tm, kd), lambda i, j: (i, 0)),
                  pl.BlockSpec((None, kd, tn), lambda i, j: (li, 0, j)),
                  pl.BlockSpec((tm, tn), lambda i, j: (i, j))],
        out_specs=pl.BlockSpec((tm, tn), lambda i, j: (i, j)),
        out_shape=jax.ShapeDtypeStruct((m, n), F32),
        compiler_params=_params(("arbitrary", "arbitrary"), vmem),
        name="matmul_residual",
    )(a, w, x)


def _swa_kernel(sink_ref, q_ref, kc_ref, kp_ref, vc_ref, vp_ref, o_ref, *, group, hps):
    i, hb = pl.program_id(0), pl.program_id(1)
    blk = SWA_BLOCK
    nsub = q_ref.shape[0] // blk
    row = lax.broadcasted_iota(jnp.int32, (blk, 2 * blk), 0)
    col = lax.broadcasted_iota(jnp.int32, (blk, 2 * blk), 1)
    rel = col - blk
    band = (rel <= row) & (row - rel < blk)
    nt = (((1,), (1,)), ((), ()))
    for c in range(nsub):
        rows = slice(c * blk, (c + 1) * blk)
        if c == 0:
            mask = band & (rel >= jnp.where(i > 0, -blk, 0))
        else:
            prev = slice((c - 1) * blk, c * blk)
            mask = band
        for hh in range(hps):
            kcols = slice(hh * HEAD_DIM, (hh + 1) * HEAD_DIM)
            if c == 0:
                kprev, vprev = kp_ref[:, kcols], vp_ref[:, kcols]
            else:
                kprev, vprev = kc_ref[prev, kcols], vc_ref[prev, kcols]
            kw = jnp.concatenate([kprev, kc_ref[rows, kcols]], axis=0)
            vw = jnp.concatenate([vprev, vc_ref[rows, kcols]], axis=0)
            qcol = lambda g: slice((hh * group + g) * HEAD_DIM, (hh * group + g + 1) * HEAD_DIM)
            qs = jnp.concatenate([q_ref[rows, qcol(g)] for g in range(group)], axis=0)
            s = lax.dot_general(qs, kw, nt, preferred_element_type=F32)
            probs, inv = [], []
            for g in range(group):
                sg = jnp.where(mask, s[g * blk:(g + 1) * blk], -jnp.inf)
                sink = sink_ref[(hb * hps + hh) * group + g]
                mx = jnp.maximum(jnp.max(sg, axis=-1, keepdims=True), sink)
                p = jnp.exp(sg - mx)
                inv.append(1.0 / (jnp.sum(p, axis=-1, keepdims=True) + jnp.exp(sink - mx)))
                probs.append(p.astype(BF16))
            o = jnp.dot(jnp.concatenate(probs, axis=0), vw, preferred_element_type=F32)
            for g in range(group):
                o_ref[rows, qcol(g)] = (o[g * blk:(g + 1) * blk] * inv[g]).astype(o_ref.dtype)


def _swa_attention(qkv, sinks, *, hq, hkv, tq_pref=512):
    s = qkv.shape[0]
    group = hq // hkv
    hps = next(c for c in (4, 2, 1) if hkv % c == 0)
    tq = _tile(s, tq_pref, SWA_BLOCK)
    r = tq // SWA_BLOCK
    kw = hps * HEAD_DIM
    cur = lambda off: pl.BlockSpec((tq, kw), lambda i, h, sk: (i, off // hps + h))
    prev = lambda off: pl.BlockSpec((SWA_BLOCK, kw), lambda i, h, sk: (jnp.maximum(i * r - 1, 0), off // hps + h))
    vmem = (4 * tq * hps * group * HEAD_DIM * 2 + 8 * tq * kw * 2
            + 8 * hps * group * SWA_BLOCK * 2 * SWA_BLOCK * 4)
    return pl.pallas_call(
        functools.partial(_swa_kernel, group=group, hps=hps),
        grid_spec=pltpu.PrefetchScalarGridSpec(
            num_scalar_prefetch=1,
            grid=(s // tq, hkv // hps),
            in_specs=[pl.BlockSpec((tq, hps * group * HEAD_DIM), lambda i, h, sk: (i, h)),
                      cur(hq), prev(hq), cur(hq + hkv), prev(hq + hkv)],
            out_specs=pl.BlockSpec((tq, hps * group * HEAD_DIM), lambda i, h, sk: (i, h)),
        ),
        out_shape=jax.ShapeDtypeStruct((s, hq * HEAD_DIM), BF16),
        compiler_params=_params(("arbitrary", "arbitrary"), vmem),
        name="swa_attention",
    )(sinks, qkv, qkv, qkv, qkv, qkv)


def _diff_kernel(lam_ref, subln_ref, q_ref, k_ref, v_ref, o_ref, qs_ref, m_ref, l_ref, acc_ref, sa_ref, sb_ref,
                 *, group, tk, lambda_init):
    qi = pl.program_id(1)
    tq = q_ref.shape[0]
    rows = group * tq
    nt = (((1,), (1,)), ((), ()))

    for mp in range(2):
        for g in range(group):
            c = (g * 2 + mp) * HEAD_DIM
            qs_ref[mp, g * tq:(g + 1) * tq, :] = q_ref[:, c:c + HEAD_DIM]
    m_ref[...] = jnp.full(m_ref.shape, -jnp.inf, F32)
    l_ref[...] = jnp.zeros(l_ref.shape, F32)
    acc_ref[...] = jnp.zeros(acc_ref.shape, F32)

    def scores(kb, dst_ref):
        st = pl.multiple_of(kb * tk, tk)
        for mp in range(2):
            dst_ref[mp] = lax.dot_general(qs_ref[mp], k_ref[pl.ds(st, tk), mp * HEAD_DIM:(mp + 1) * HEAD_DIM], nt,
                                          preferred_element_type=F32)

    def update(kb, src_ref, nkeys, masked):
        st = pl.multiple_of(kb * tk, tk)
        vblk = v_ref[pl.ds(st, nkeys), :]
        if masked:
            qpos = qi * tq + (lax.broadcasted_iota(jnp.int32, (rows, nkeys), 0) & (tq - 1))
            kpos = st + lax.broadcasted_iota(jnp.int32, (rows, nkeys), 1)
            visible = kpos <= qpos
        for mp in range(2):
            s = src_ref[mp, :, :nkeys]
            if masked:
                s = jnp.where(visible, s, -jnp.inf)
            m_prev = m_ref[mp]
            m_next = jnp.maximum(m_prev, jnp.max(s, axis=-1, keepdims=True))
            alpha = jnp.exp2(m_prev - m_next)
            p = jnp.exp2(s - pltpu.repeat(m_next, nkeys // LANES, 1))
            lane_sum = p[:, :LANES]
            for t in range(1, nkeys // LANES):
                lane_sum = lane_sum + p[:, t * LANES:(t + 1) * LANES]
            l_ref[mp] = alpha * l_ref[mp] + lane_sum
            pv = jnp.dot(p.astype(BF16), vblk, preferred_element_type=F32)
            acc_ref[mp] = acc_ref[mp] * pltpu.repeat(alpha, acc_ref.shape[2] // LANES, 1) + pv
            m_ref[mp] = m_next

    n_full = (qi * tq) // tk
    scores(0, sa_ref)

    def body(t, carry):
        @pl.when(t % 2 == 0)
        def _():
            scores(t + 1, sb_ref)
            update(t, sa_ref, tk, False)

        @pl.when(t % 2 == 1)
        def _():
            scores(t + 1, sa_ref)
            update(t, sb_ref, tk, False)

        return carry

    lax.fori_loop(0, n_full, body, 0)

    sub = qi % (tk // tq)
    for k in range(tk // tq):
        for parity, buf in enumerate((sa_ref, sb_ref)):
            @pl.when((sub == k) & (n_full % 2 == parity))
            def _():
                update(n_full, buf, (k + 1) * tq, True)

    lam_vecs = lam_ref[...]
    lam = (jnp.exp(jnp.sum(lam_vecs[0:1] * lam_vecs[1:2], axis=-1, keepdims=True))
           - jnp.exp(jnp.sum(lam_vecs[2:3] * lam_vecs[3:4], axis=-1, keepdims=True)) + lambda_init)
    o = (acc_ref[0] / jnp.sum(l_ref[0], axis=-1, keepdims=True)
         - lam * (acc_ref[1] / jnp.sum(l_ref[1], axis=-1, keepdims=True)))
    y = _rms(o, subln_ref[...]) * (1.0 - lambda_init)
    vd = y.shape[1]
    for g in range(group):
        o_ref[:, g * vd:(g + 1) * vd] = y[g * tq:(g + 1) * tq].astype(o_ref.dtype)


def _diff_attention(qkv, lam_vecs, subln, li, *, heads, hkv, lambda_init, tq_pref=256, tk_pref=1024):
    s = qkv.shape[0]
    group = heads // hkv
    vd = 2 * HEAD_DIM
    tq = _tile(s, tq_pref, 8)
    tk = _tile(s, tk_pref, LANES)
    assert tk % tq == 0 and tq & (tq - 1) == 0, (tq, tk)
    rows = group * tq
    vmem = (2 * s * vd * 2 + 4 * tq * group * vd * 2 + 2 * rows * HEAD_DIM * 2
            + 4 * rows * LANES * 4 + 2 * rows * vd * 4 + 4 * rows * tk * 4 + 2 * rows * tk * 4)
    kern = functools.partial(_diff_kernel, group=group, tk=tk, lambda_init=lambda_init)
    resident = lambda off: pl.BlockSpec((s, vd), lambda h, i: (0, off + h), pipeline_mode=pl.Buffered(1))
    return pl.pallas_call(
        kern,
        grid=(hkv, s // tq),
        in_specs=[pl.BlockSpec((4, HEAD_DIM), lambda h, i: (0, 0)),
                  pl.BlockSpec((None, 1, vd), lambda h, i: (li, 0, 0)),
                  pl.BlockSpec((tq, group * vd), lambda h, i: (i, h)),
                  resident(heads), resident(heads + hkv)],
        out_specs=pl.BlockSpec((tq, group * vd), lambda h, i: (i, h)),
        out_shape=jax.ShapeDtypeStruct((s, heads * vd), BF16),
        scratch_shapes=[pltpu.VMEM((2, rows, HEAD_DIM), BF16),
                        pltpu.VMEM((2, rows, LANES), F32),
                        pltpu.VMEM((2, rows, LANES), F32),
                        pltpu.VMEM((2, rows, vd), F32),
                        pltpu.VMEM((2, rows, tk), F32),
                        pltpu.VMEM((2, rows, tk), F32)],
        compiler_params=_params(("arbitrary", "arbitrary"), vmem),
        name="diff_attention",
    )(lam_vecs, _row(subln), qkv, qkv, qkv)


def _pool_kernel(x_ref, xp_ref, g_ref, w_ref, scale_ref, o_ref, ext_ref):
    i = pl.program_id(0)
    tm = x_ref.shape[0]
    ch = w_ref.shape[1]
    x = x_ref[...]
    h = _rms(x, g_ref[...])
    hp = _rms(xp_ref[...], g_ref[...])
    ext_ref[0:POOL_HALO, :] = jnp.where(i > 0, hp, 0.0)
    ext_ref[POOL_HALO:, :] = h
    t = i * tm + lax.broadcasted_iota(jnp.int32, (tm, 1), 0)
    for gi, win in enumerate(POOL_WINDOWS):
        cols = slice(gi * ch, (gi + 1) * ch)
        e = ext_ref[:, cols]
        span = 1
        while span < win:
            e = e + pltpu.roll(e, span, 0)
            span *= 2
        count = jnp.minimum(t + 1, win).astype(F32)
        pooled = (e[POOL_HALO:] / count - h[:, cols]).astype(BF16)
        y = jnp.dot(pooled, w_ref[gi], preferred_element_type=F32)
        o_ref[:, cols] = x[:, cols] + y * scale_ref[:, cols]


def _pool_mixer(x, g, w, scale, li, *, tm_pref=256):
    s, d = x.shape
    groups, ch = w.shape[1], w.shape[2]
    assert groups == len(POOL_WINDOWS)
    tm = _tile(s, tm_pref, POOL_HALO)
    r = tm // POOL_HALO
    vec = pl.BlockSpec((None, 1, d), lambda i: (li, 0, 0))
    vmem = 4 * tm * d * 4 + 2 * groups * ch * ch * 2 + (tm + POOL_HALO) * d * 4 + 6 * tm * ch * 4
    return pl.pallas_call(
        _pool_kernel,
        grid=(s // tm,),
        in_specs=[pl.BlockSpec((tm, d), lambda i: (i, 0)),
                  pl.BlockSpec((POOL_HALO, d), lambda i: (jnp.maximum(i * r - 1, 0), 0)),
                  vec,
                  pl.BlockSpec((None, groups, ch, ch), lambda i: (li, 0, 0, 0)),
                  vec],
        out_specs=pl.BlockSpec((tm, d), lambda i: (i, 0)),
        out_shape=jax.ShapeDtypeStruct((s, d), F32),
        scratch_shapes=[pltpu.VMEM((tm + POOL_HALO, d), F32)],
        compiler_params=_params(("arbitrary",), vmem),
        name="pool_mixer",
    )(x, x, _row(g), w, _row(scale))


def _xattn_kernel(x_ref, g_ref, wq_ref, kv_ref, wo_ref, gn_ref, o_ref, hn_ref, *, heads):
    x = x_ref[...]
    h = _rms(x, g_ref[...]).astype(BF16)
    width = wq_ref.shape[1]
    hd = width // heads
    q = (jnp.dot(h, wq_ref[...], preferred_element_type=F32) * (hd ** -0.5)).astype(BF16)
    nt = (((1,), (1,)), ((), ()))
    outs = []
    for a in range(heads):
        cols = slice(a * hd, (a + 1) * hd)
        s = lax.dot_general(q[:, cols], kv_ref[:, cols], nt, preferred_element_type=F32)
        p = jnp.exp(s - jnp.max(s, axis=-1, keepdims=True))
        den = jnp.sum(p, axis=-1, keepdims=True)
        pv = jnp.dot(p.astype(BF16), kv_ref[:, width + a * hd:width + (a + 1) * hd], preferred_element_type=F32)
        outs.append((pv / den).astype(BF16))
    o = jnp.concatenate(outs, axis=-1)
    y = x + jnp.dot(o, wo_ref[...], preferred_element_type=F32)
    o_ref[...] = y
    hn_ref[...] = _rms(y, gn_ref[...]).astype(hn_ref.dtype)


def _xattn(x, g, wq, kv, wo, g_next, li, *, tm_pref=256):
    s, d = x.shape
    width = wq.shape[-1]
    mem_len = kv.shape[0]
    tm = _tile(s, tm_pref, 8)
    vmem = (4 * tm * d * 4 + 2 * tm * d * 2 + 2 * d * width * 2 + mem_len * 2 * width * 2 + tm * d * 6
            + 4 * tm * width * 4)
    layer = lambda shape: pl.BlockSpec((None,) + shape, lambda i: (li, 0, 0), pipeline_mode=pl.Buffered(1))
    rows = pl.BlockSpec((tm, d), lambda i: (i, 0))
    return pl.pallas_call(
        functools.partial(_xattn_kernel, heads=X_HEADS),
        grid=(s // tm,),
        in_specs=[rows, layer((1, d)), layer((d, width)),
                  pl.BlockSpec((mem_len, 2 * width), lambda i: (0, 0), pipeline_mode=pl.Buffered(1)),
                  layer((width, d)), layer((1, d))],
        out_specs=[rows, rows],
        out_shape=[jax.ShapeDtypeStruct((s, d), F32), jax.ShapeDtypeStruct((s, d), BF16)],
        compiler_params=_params(("arbitrary",), vmem),
        name="xattn",
    )(x, _row(g), wq, kv, wo, _row(g_next))


def _ffn_in_kernel(h_ref, wg_ref, wv_ref, cwg_ref, cwv_ref, cbg_ref, cbv_ref, o_ref, cg_ref, cv_ref, *, chunk):
    i, j = pl.program_id(0), pl.program_id(1)
    tm = h_ref.shape[0]

    @pl.when(i == 0)
    def _():
        cg_ref[j] = jnp.zeros(cg_ref.shape[1:], F32)
        cv_ref[j] = jnp.zeros(cv_ref.shape[1:], F32)

    def conv(w_ref, cw_ref, cb_ref, carry_ref):
        u = jnp.dot(h_ref[...], w_ref[...], preferred_element_type=F32)
        first = carry_ref[j]
        carry_ref[j] = u[tm - CONV_HALO:, :]
        out = []
        for r0 in range(0, tm, chunk):
            prev = first if r0 == 0 else u[r0 - CONV_HALO:r0, :]
            ext = jnp.concatenate([prev, u[r0:r0 + chunk, :]], axis=0)
            c = cb_ref[...]
            for tap in range(CONV_WIDTH):
                lag = CONV_WIDTH - 1 - tap
                shifted = pltpu.roll(ext, lag, 0) if lag else ext
                c = c + cw_ref[tap:tap + 1, :] * shifted[CONV_HALO:, :]
            out.append(c)
        return out

    gate = conv(wg_ref, cwg_ref, cbg_ref, cg_ref)
    val = conv(wv_ref, cwv_ref, cbv_ref, cv_ref)
    for k, (gk, vk) in enumerate(zip(gate, val)):
        o_ref[k * chunk:(k + 1) * chunk, :] = (gk / (1.0 + jnp.exp(-gk)) * vk).astype(o_ref.dtype)


def _ffn_in(h, w_in, conv_w, conv_b, li, *, tm_pref=1024, tn_pref=512, chunk_pref=256):
    m, d = h.shape
    f = w_in.shape[-1] // 2
    tm, tn = _tile(m, tm_pref, 8), _tile(f, tn_pref, LANES)
    chunk = _tile(tm, chunk_pref, 8)
    nj = f // tn
    lo = lambda rows: pl.BlockSpec((None, rows, tn), lambda i, j: (li, 0, j))
    hi = lambda rows: pl.BlockSpec((None, rows, tn), lambda i, j: (li, 0, j + nj))
    vmem = 2 * tm * d * 2 + 4 * d * tn * 2 + 2 * tm * tn * 2 + 2 * nj * CONV_HALO * tn * 4 + 6 * tm * tn * 4
    return pl.pallas_call(
        functools.partial(_ffn_in_kernel, chunk=chunk),
        grid=(m // tm, nj),
        in_specs=[pl.BlockSpec((tm, d), lambda i, j: (i, 0)),
                  lo(d), hi(d), lo(CONV_WIDTH), hi(CONV_WIDTH), lo(1), hi(1)],
        out_specs=pl.BlockSpec((tm, tn), lambda i, j: (i, j)),
        out_shape=jax.ShapeDtypeStruct((m, f), BF16),
        scratch_shapes=[pltpu.VMEM((nj, CONV_HALO, tn), F32),
                        pltpu.VMEM((nj, CONV_HALO, tn), F32)],
        compiler_params=_params(("arbitrary", "arbitrary"), vmem),
        name="ffn_in",
    )(h, w_in, w_in, conv_w, conv_w, _row(conv_b), _row(conv_b))


def _final_norm_kernel(x_ref, g_ref, o_ref):
    o_ref[...] = _rms(x_ref[...], g_ref[...])


def _final_norm(x, g, *, tm_pref=512):
    s, d = x.shape
    tm = _tile(s, tm_pref, 8)
    return pl.pallas_call(
        _final_norm_kernel,
        grid=(s // tm,),
        in_specs=[pl.BlockSpec((tm, d), lambda i: (i, 0)), pl.BlockSpec((1, d), lambda i: (0, 0))],
        out_specs=pl.BlockSpec((tm, d), lambda i: (i, 0)),
        out_shape=jax.ShapeDtypeStruct((s, d), F32),
        compiler_params=_params(("arbitrary",), 6 * tm * d * 4),
        name="final_norm",
    )(x, g.reshape(1, d))


def kernel(x, mem, positions, swa_norm, swa_w_qkv, swa_sinks, swa_w_o, diff_norm, diff_w_qkv, diff_lambda_q1, diff_lambda_k1, diff_lambda_q2, diff_lambda_k2, diff_subln, diff_w_o, pool_norm, pool_w, pool_scale, xattn_norm, xattn_mem_norm, xattn_w_q, xattn_w_kv, xattn_w_o, ffn_norm, ffn_w_in, ffn_conv_w, ffn_conv_b, ffn_w_out, final_norm):
    batch, seq, d = x.shape
    assert batch == 1, "the kernels fold no batch dimension"
    depth = ffn_w_in.shape[0]
    xs = x.reshape(seq, d)
    mem2 = mem.reshape(mem.shape[1], d)
    tables = _rope_tables(positions)
    bf = lambda w: w.astype(BF16)
    swa_w_qkv, swa_w_o, diff_w_qkv, diff_w_o, pool_w = map(bf, (swa_w_qkv, swa_w_o, diff_w_qkv, diff_w_o, pool_w))
    xattn_w_q, xattn_w_kv, xattn_w_o, ffn_w_in, ffn_w_out = map(
        bf, (xattn_w_q, xattn_w_kv, xattn_w_o, ffn_w_in, ffn_w_out))

    swa_hq = d // HEAD_DIM
    swa_hkv = (swa_w_qkv.shape[-1] // HEAD_DIM - swa_hq) // 2
    diff_heads = d // (2 * HEAD_DIM)
    diff_hkv = (diff_w_qkv.shape[-1] // (2 * HEAD_DIM) - diff_heads) // 2

    for i in range(depth):
        kind, j = i % N_MIXERS, i // N_MIXERS
        if kind == 0:
            qkv = _qkv_rope(xs, swa_norm, swa_w_qkv, j, tables, n_q=swa_hq * HEAD_DIM, n_k=swa_hkv * HEAD_DIM,
                            q_variant=ROPE_Q_SWA)
            att = _swa_attention(qkv, swa_sinks[j], hq=swa_hq, hkv=swa_hkv)
            xs = _matmul_residual(att, swa_w_o, j, xs, tm_pref=1024, tn_pref=512)
        elif kind == 1:
            lambda_init = 0.8 - 0.6 * math.exp(-0.3 * i)
            qkv = _qkv_rope(xs, diff_norm, diff_w_qkv, j, tables, n_q=2 * diff_heads * HEAD_DIM,
                            n_k=2 * diff_hkv * HEAD_DIM, q_variant=ROPE_Q_DIFF)
            lam_vecs = jnp.stack([diff_lambda_q1[j], diff_lambda_k1[j], diff_lambda_q2[j], diff_lambda_k2[j]])
            att = _diff_attention(qkv, lam_vecs, diff_subln, j, heads=diff_heads, hkv=diff_hkv,
                                  lambda_init=lambda_init)
            xs = _matmul_residual(att, diff_w_o, j, xs, tm_pref=1024, tn_pref=512)
        else:
            xs = _pool_mixer(xs, pool_norm, pool_w, pool_scale, j)
        kv = _norm_matmul(mem2, xattn_mem_norm, xattn_w_kv, i)
        xs, hn = _xattn(xs, xattn_norm, xattn_w_q, kv, xattn_w_o, ffn_norm, i)
        act = _ffn_in(hn, ffn_w_in, ffn_conv_w, ffn_conv_b, i)
        xs = _matmul_residual(act, ffn_w_out, i, xs, tm_pref=512, tn_pref=512)
    return _final_norm(xs, final_norm).reshape(batch, seq, d)
```

```python
import functools
import math

import jax
import jax.numpy as jnp
from jax import lax
from jax.experimental import pallas as pl
from jax.experimental.pallas import tpu as pltpu

HEAD_DIM = 128
ROPE_DIM = HEAD_DIM // 4
ROPE_HALF = ROPE_DIM // 2
ROPE_THETA = 500000.0
SWA_BLOCK = 128
POOL_WINDOWS = (2, 4, 8, 16)
POOL_HALO = 16
X_HEADS = 4
CONV_WIDTH = 3
CONV_HALO = 8
N_MIXERS = 3
NORM_EPS = 1e-5
LANES = 128
LOG2E = math.log2(math.e)

V7X_VMEM_BYTES = 64 * 2**20
VMEM_RESERVE_BYTES = 6 * 2**20

F32 = jnp.float32
BF16 = jnp.bfloat16

ROPE_Q_SWA, ROPE_Q_DIFF, ROPE_K, ROPE_NONE = 0, 1, 2, 3
ROPE_SCALES = (HEAD_DIM ** -0.5, HEAD_DIM ** -0.5 * LOG2E, 1.0)


def _params(semantics, vmem_bytes):
    limit = min(int(vmem_bytes) * 5 // 4 + VMEM_RESERVE_BYTES, V7X_VMEM_BYTES - VMEM_RESERVE_BYTES // 2)
    return pltpu.CompilerParams(dimension_semantics=semantics, vmem_limit_bytes=limit)


def _tile(dim, pref, align):
    if dim <= pref:
        return dim
    t = (pref // align) * align
    while t > align and dim % t:
        t -= align
    assert dim % t == 0, (dim, pref, align)
    return t


def _rms(x, g, eps=NORM_EPS):
    ms = jnp.mean(x * x, axis=-1, keepdims=True)
    return x * lax.rsqrt(ms + eps) * g


def _row(stack):
    return stack.reshape(stack.shape[0], 1, stack.shape[-1])


def _rope_table_kernel(pos_ref, invf_ref, cos_ref, sa_ref, sb_ref):
    ang = pos_ref[...].astype(F32) * invf_ref[...]
    lane = lax.broadcasted_iota(jnp.int32, ang.shape, 1)
    s = jnp.sin(ang)
    cos = jnp.cos(ang)
    sa = jnp.where(lane >= ROPE_HALF, s, 0.0)
    sb = jnp.where(lane < ROPE_HALF, -s, 0.0)
    for v, scale in enumerate(ROPE_SCALES):
        cos_ref[v], sa_ref[v], sb_ref[v] = cos * scale, sa * scale, sb * scale
    cos_ref[ROPE_NONE] = jnp.ones(cos.shape, F32)
    sa_ref[ROPE_NONE] = jnp.zeros(cos.shape, F32)
    sb_ref[ROPE_NONE] = jnp.zeros(cos.shape, F32)


def _rope_tables(positions):
    s = positions.shape[-1]
    inv_freq = ROPE_THETA ** (-jnp.arange(0, ROPE_DIM, 2, dtype=F32) / ROPE_DIM)
    invf = jnp.zeros((1, LANES), F32).at[0, :ROPE_DIM].set(jnp.concatenate([inv_freq, inv_freq]))
    nv = len(ROPE_SCALES) + 1
    tm = _tile(s, 1024, 8)
    tab = jax.ShapeDtypeStruct((nv, s, LANES), F32)
    return pl.pallas_call(
        _rope_table_kernel,
        grid=(s // tm,),
        in_specs=[pl.BlockSpec((tm, 1), lambda i: (i, 0)), pl.BlockSpec((1, LANES), lambda i: (0, 0))],
        out_specs=[pl.BlockSpec((nv, tm, LANES), lambda i: (0, i, 0))] * 3,
        out_shape=[tab] * 3,
        compiler_params=_params(("arbitrary",), (6 * nv + 8) * tm * LANES * 4),
        name="rope_tables",
    )(positions.reshape(s, 1), invf)


def _norm_to_scratch(x_ref, g_ref, h_ref):
    @pl.when(pl.program_id(1) == 0)
    def _():
        h_ref[...] = _rms(x_ref[...], g_ref[...]).astype(h_ref.dtype)


def _norm_matmul_kernel(x_ref, g_ref, w_ref, o_ref, h_ref):
    _norm_to_scratch(x_ref, g_ref, h_ref)
    o_ref[...] = jnp.dot(h_ref[...], w_ref[...], preferred_element_type=F32).astype(o_ref.dtype)


def _norm_matmul(x, g, w, li, *, tm_pref=512, tn_pref=1024):
    m, d = x.shape
    n = w.shape[-1]
    tm, tn = _tile(m, tm_pref, 8), _tile(n, tn_pref, LANES)
    vmem = 2 * tm * d * 4 + tm * d * 2 + 2 * d * tn * 2 + 2 * tm * tn * 2 + 2 * tm * tn * 4
    return pl.pallas_call(
        _norm_matmul_kernel,
        grid=(m // tm, n // tn),
        in_specs=[pl.BlockSpec((tm, d), lambda i, j: (i, 0)),
                  pl.BlockSpec((None, 1, d), lambda i, j: (li, 0, 0)),
                  pl.BlockSpec((None, d, tn), lambda i, j: (li, 0, j))],
        out_specs=pl.BlockSpec((tm, tn), lambda i, j: (i, j)),
        out_shape=jax.ShapeDtypeStruct((m, n), BF16),
        scratch_shapes=[pltpu.VMEM((tm, d), BF16)],
        compiler_params=_params(("arbitrary", "arbitrary"), vmem),
        name="norm_matmul",
    )(x, _row(g), w)


def _qkv_rope_kernel(x_ref, g_ref, w_ref, cos_ref, sa_ref, sb_ref, o_ref, h_ref):
    _norm_to_scratch(x_ref, g_ref, h_ref)
    y = jnp.dot(h_ref[...], w_ref[...], preferred_element_type=F32)
    cos, sa, sb = cos_ref[...], sa_ref[...], sb_ref[...]
    for c in range(y.shape[1] // HEAD_DIM):
        sl = slice(c * HEAD_DIM, (c + 1) * HEAD_DIM)
        t = y[:, sl]
        r = t * cos + pltpu.roll(t, ROPE_HALF, 1) * sa + pltpu.roll(t, HEAD_DIM - ROPE_HALF, 1) * sb
        o_ref[:, sl] = r.astype(o_ref.dtype)


def _qkv_rope(x, g, w, li, tables, *, n_q, n_k, q_variant, tm_pref=512):
    m, d = x.shape
    n = w.shape[-1]
    tn = _tile(math.gcd(n_q, n_k, n - n_q - n_k), 1024, HEAD_DIM)
    tm = _tile(m, tm_pref, 8)
    nqb, nkb = n_q // tn, n_k // tn

    def tab_index(i, j):
        return (jnp.where(j < nqb, q_variant, jnp.where(j < nqb + nkb, ROPE_K, ROPE_NONE)), i, 0)

    tab_spec = pl.BlockSpec((None, tm, LANES), tab_index)
    vmem = 2 * tm * d * 4 + tm * d * 2 + 2 * d * tn * 2 + 2 * tm * tn * 2 + 3 * tm * tn * 4
    return pl.pallas_call(
        _qkv_rope_kernel,
        grid=(m // tm, n // tn),
        in_specs=[pl.BlockSpec((tm, d), lambda i, j: (i, 0)),
                  pl.BlockSpec((None, 1, d), lambda i, j: (li, 0, 0)),
                  pl.BlockSpec((None, d, tn), lambda i, j: (li, 0, j)),
                  tab_spec, tab_spec, tab_spec],
        out_specs=pl.BlockSpec((tm, tn), lambda i, j: (i, j)),
        out_shape=jax.ShapeDtypeStruct((m, n), BF16),
        scratch_shapes=[pltpu.VMEM((tm, d), BF16)],
        compiler_params=_params(("arbitrary", "arbitrary"), vmem),
        name="qkv_rope",
    )(x, _row(g), w, *tables)


def _matmul_residual_kernel(a_ref, w_ref, x_ref, o_ref):
    o_ref[...] = x_ref[...] + jnp.dot(a_ref[...], w_ref[...], preferred_element_type=F32)


def _matmul_residual(a, w, li, x, *, tm_pref, tn_pref):
    m, kd = a.shape
    n = w.shape[-1]
    tm, tn = _tile(m, tm_pref, 8), _tile(n, tn_pref, LANES)
    vmem = 2 * tm * kd * 2 + 2 * kd * tn * 2 + 5 * tm * tn * 4
    return pl.pallas_call(
        _matmul_residual_kernel,
        grid=(m // tm, n // tn),
        in_specs=[pl.BlockSpec((tm, kd), lambda i, j: (i, 0)),
                  pl.BlockSpec((None, kd, tn), lambda i, j: (li, 0, j)),
                  pl.BlockSpec((tm, tn), lambda i, j: (i, j))],
        out_specs=pl.BlockSpec((tm, tn), lambda i, j: (i, j)),
        out_shape=jax.ShapeDtypeStruct((m, n), F32),
        compiler_params=_params(("arbitrary", "arbitrary"), vmem),
        name="matmul_residual",
    )(a, w, x)


def _swa_kernel(sink_ref, q_ref, kc_ref, kp_ref, vc_ref, vp_ref, o_ref, *, group, hps):
    i, hb = pl.program_id(0), pl.program_id(1)
    blk = SWA_BLOCK
    nsub = q_ref.shape[0] // blk
    row = lax.broadcasted_iota(jnp.int32, (blk, 2 * blk), 0)
    col = lax.broadcasted_iota(jnp.int32, (blk, 2 * blk), 1)
    rel = col - blk
    band = (rel <= row) & (row - rel < blk)
    nt = (((1,), (1,)), ((), ()))
    for c in range(nsub):
        rows = slice(c * blk, (c + 1) * blk)
        if c == 0:
            mask = band & (rel >= jnp.where(i > 0, -blk, 0))
        else:
            prev = slice((c - 1) * blk, c * blk)
            mask = band
        for hh in range(hps):
            kcols = slice(hh * HEAD_DIM, (hh + 1) * HEAD_DIM)
            if c == 0:
                kprev, vprev = kp_ref[:, kcols], vp_ref[:, kcols]
            else:
                kprev, vprev = kc_ref[prev, kcols], vc_ref[prev, kcols]
            kw = jnp.concatenate([kprev, kc_ref[rows, kcols]], axis=0)
            vw = jnp.concatenate([vprev, vc_ref[rows, kcols]], axis=0)
            qcol = lambda g: slice((hh * group + g) * HEAD_DIM, (hh * group + g + 1) * HEAD_DIM)
            qs = jnp.concatenate([q_ref[rows, qcol(g)] for g in range(group)], axis=0)
            s = lax.dot_general(qs, kw, nt, preferred_element_type=F32)
            probs, inv = [], []
            for g in range(group):
                sg = jnp.where(mask, s[g * blk:(g + 1) * blk], -jnp.inf)
                sink = sink_ref[(hb * hps + hh) * group + g]
                mx = jnp.maximum(jnp.max(sg, axis=-1, keepdims=True), sink)
                p = jnp.exp(sg - mx)
                inv.append(1.0 / (jnp.sum(p, axis=-1, keepdims=True) + jnp.exp(sink - mx)))
                probs.append(p.astype(BF16))
            o = jnp.dot(jnp.concatenate(probs, axis=0), vw, preferred_element_type=F32)
            for g in range(group):
                o_ref[rows, qcol(g)] = (o[g * blk:(g + 1) * blk] * inv[g]).astype(o_ref.dtype)


def _swa_attention(qkv, sinks, *, hq, hkv, tq_pref=512):
    s = qkv.shape[0]
    group = hq // hkv
    hps = next(c for c in (4, 2, 1) if hkv % c == 0)
    tq = _tile(s, tq_pref, SWA_BLOCK)
    r = tq // SWA_BLOCK
    kw = hps * HEAD_DIM
    cur = lambda off: pl.BlockSpec((tq, kw), lambda i, h, sk: (i, off // hps + h))
    prev = lambda off: pl.BlockSpec((SWA_BLOCK, kw), lambda i, h, sk: (jnp.maximum(i * r - 1, 0), off // hps + h))
    vmem = (4 * tq * hps * group * HEAD_DIM * 2 + 8 * tq * kw * 2
            + 8 * hps * group * SWA_BLOCK * 2 * SWA_BLOCK * 4)
    return pl.pallas_call(
        functools.partial(_swa_kernel, group=group, hps=hps),
        grid_spec=pltpu.PrefetchScalarGridSpec(
            num_scalar_prefetch=1,
            grid=(s // tq, hkv // hps),
            in_specs=[pl.BlockSpec((tq, hps * group * HEAD_DIM), lambda i, h, sk: (i, h)),
                      cur(hq), prev(hq), cur(hq + hkv), prev(hq + hkv)],
            out_specs=pl.BlockSpec((tq, hps * group * HEAD_DIM), lambda i, h, sk: (i, h)),
        ),
        out_shape=jax.ShapeDtypeStruct((s, hq * HEAD_DIM), BF16),
        compiler_params=_params(("arbitrary", "arbitrary"), vmem),
        name="swa_attention",
    )(sinks, qkv, qkv, qkv, qkv, qkv)


def _diff_kernel(lam_ref, subln_ref, q_ref, k_ref, v_ref, o_ref, qs_ref, m_ref, l_ref, acc_ref, sa_ref, sb_ref,
                 *, group, tk, lambda_init):
    qi = pl.program_id(1)
    tq = q_ref.shape[0]
    rows = group * tq
    nt = (((1,), (1,)), ((), ()))

    for mp in range(2):
        for g in range(group):
            c = (g * 2 + mp) * HEAD_DIM
            qs_ref[mp, g * tq:(g + 1) * tq, :] = q_ref[:, c:c + HEAD_DIM]
    m_ref[...] = jnp.full(m_ref.shape, -jnp.inf, F32)
    l_ref[...] = jnp.zeros(l_ref.shape, F32)
    acc_ref[...] = jnp.zeros(acc_ref.shape, F32)

    def scores(kb, dst_ref):
        st = pl.multiple_of(kb * tk, tk)
        for mp in range(2):
            dst_ref[mp] = lax.dot_general(qs_ref[mp], k_ref[pl.ds(st, tk), mp * HEAD_DIM:(mp + 1) * HEAD_DIM], nt,
                                          preferred_element_type=F32)

    def update(kb, src_ref, nkeys, masked):
        st = pl.multiple_of(kb * tk, tk)
        vblk = v_ref[pl.ds(st, nkeys), :]
        if masked:
            qpos = qi * tq + (lax.broadcasted_iota(jnp.int32, (rows, nkeys), 0) & (tq - 1))
            kpos = st + lax.broadcasted_iota(jnp.int32, (rows, nkeys), 1)
            visible = kpos <= qpos
        for mp in range(2):
            s = src_ref[mp, :, :nkeys]
            if masked:
                s = jnp.where(visible, s, -jnp.inf)
            m_prev = m_ref[mp]
            m_next = jnp.maximum(m_prev, jnp.max(s, axis=-1, keepdims=True))
            alpha = jnp.exp2(m_prev - m_next)
            p = jnp.exp2(s - pltpu.repeat(m_next, nkeys // LANES, 1))
            lane_sum = p[:, :LANES]
            for t in range(1, nkeys // LANES):
                lane_sum = lane_sum + p[:, t * LANES:(t + 1) * LANES]
            l_ref[mp] = alpha * l_ref[mp] + lane_sum
            pv = jnp.dot(p.astype(BF16), vblk, preferred_element_type=F32)
            acc_ref[mp] = acc_ref[mp] * pltpu.repeat(alpha, acc_ref.shape[2] // LANES, 1) + pv
            m_ref[mp] = m_next

    n_full = (qi * tq) // tk
    scores(0, sa_ref)

    def body(t, carry):
        @pl.when(t % 2 == 0)
        def _():
            scores(t + 1, sb_ref)
            update(t, sa_ref, tk, False)

        @pl.when(t % 2 == 1)
        def _():
            scores(t + 1, sa_ref)
            update(t, sb_ref, tk, False)

        return carry

    lax.fori_loop(0, n_full, body, 0)

    sub = qi % (tk // tq)
    for k in range(tk // tq):
        for parity, buf in enumerate((sa_ref, sb_ref)):
            @pl.when((sub == k) & (n_full % 2 == parity))
            def _():
                update(n_full, buf, (k + 1) * tq, True)

    lam_vecs = lam_ref[...]
    lam = (jnp.exp(jnp.sum(lam_vecs[0:1] * lam_vecs[1:2], axis=-1, keepdims=True))
           - jnp.exp(jnp.sum(lam_vecs[2:3] * lam_vecs[3:4], axis=-1, keepdims=True)) + lambda_init)
    o = (acc_ref[0] / jnp.sum(l_ref[0], axis=-1, keepdims=True)
         - lam * (acc_ref[1] / jnp.sum(l_ref[1], axis=-1, keepdims=True)))
    y = _rms(o, subln_ref[...]) * (1.0 - lambda_init)
    vd = y.shape[1]
    for g in range(group):
        o_ref[:, g * vd:(g + 1) * vd] = y[g * tq:(g + 1) * tq].astype(o_ref.dtype)


def _diff_attention(qkv, lam_vecs, subln, li, *, heads, hkv, lambda_init, tq_pref=256, tk_pref=1024):
    s = qkv.shape[0]
    group = heads // hkv
    vd = 2 * HEAD_DIM
    tq = _tile(s, tq_pref, 8)
    tk = _tile(s, tk_pref, LANES)
    assert tk % tq == 0 and tq & (tq - 1) == 0, (tq, tk)
    rows = group * tq
    vmem = (2 * s * vd * 2 + 4 * tq * group * vd * 2 + 2 * rows * HEAD_DIM * 2
            + 4 * rows * LANES * 4 + 2 * rows * vd * 4 + 4 * rows * tk * 4 + 2 * rows * tk * 4)
    kern = functools.partial(_diff_kernel, group=group, tk=tk, lambda_init=lambda_init)
    resident = lambda off: pl.BlockSpec((s, vd), lambda h, i: (0, off + h), pipeline_mode=pl.Buffered(1))
    return pl.pallas_call(
        kern,
        grid=(hkv, s // tq),
        in_specs=[pl.BlockSpec((4, HEAD_DIM), lambda h, i: (0, 0)),
                  pl.BlockSpec((None, 1, vd), lambda h, i: (li, 0, 0)),
                  pl.BlockSpec((tq, group * vd), lambda h, i: (i, h)),
                  resident(heads), resident(heads + hkv)],
        out_specs=pl.BlockSpec((tq, group * vd), lambda h, i: (i, h)),
        out_shape=jax.ShapeDtypeStruct((s, heads * vd), BF16),
        scratch_shapes=[pltpu.VMEM((2, rows, HEAD_DIM), BF16),
                        pltpu.VMEM((2, rows, LANES), F32),
                        pltpu.VMEM((2, rows, LANES), F32),
                        pltpu.VMEM((2, rows, vd), F32),
                        pltpu.VMEM((2, rows, tk), F32),
                        pltpu.VMEM((2, rows, tk), F32)],
        compiler_params=_params(("arbitrary", "arbitrary"), vmem),
        name="diff_attention",
    )(lam_vecs, _row(subln), qkv, qkv, qkv)


def _pool_kernel(x_ref, xp_ref, g_ref, w_ref, scale_ref, o_ref, ext_ref):
    i = pl.program_id(0)
    tm = x_ref.shape[0]
    ch = w_ref.shape[1]
    x = x_ref[...]
    h = _rms(x, g_ref[...])
    hp = _rms(xp_ref[...], g_ref[...])
    ext_ref[0:POOL_HALO, :] = jnp.where(i > 0, hp, 0.0)
    ext_ref[POOL_HALO:, :] = h
    t = i * tm + lax.broadcasted_iota(jnp.int32, (tm, 1), 0)
    for gi, win in enumerate(POOL_WINDOWS):
        cols = slice(gi * ch, (gi + 1) * ch)
        e = ext_ref[:, cols]
        span = 1
        while span < win:
            e = e + pltpu.roll(e, span, 0)
            span *= 2
        count = jnp.minimum(t + 1, win).astype(F32)
        pooled = (e[POOL_HALO:] / count - h[:, cols]).astype(BF16)
        y = jnp.dot(pooled, w_ref[gi], preferred_element_type=F32)
        o_ref[:, cols] = x[:, cols] + y * scale_ref[:, cols]


def _pool_mixer(x, g, w, scale, li, *, tm_pref=256):
    s, d = x.shape
    groups, ch = w.shape[1], w.shape[2]
    assert groups == len(POOL_WINDOWS)
    tm = _tile(s, tm_pref, POOL_HALO)
    r = tm // POOL_HALO
    vec = pl.BlockSpec((None, 1, d), lambda i: (li, 0, 0))
    vmem = 4 * tm * d * 4 + 2 * groups * ch * ch * 2 + (tm + POOL_HALO) * d * 4 + 6 * tm * ch * 4
    return pl.pallas_call(
        _pool_kernel,
        grid=(s // tm,),
        in_specs=[pl.BlockSpec((tm, d), lambda i: (i, 0)),
                  pl.BlockSpec((POOL_HALO, d), lambda i: (jnp.maximum(i * r - 1, 0), 0)),
                  vec,
                  pl.BlockSpec((None, groups, ch, ch), lambda i: (li, 0, 0, 0)),
                  vec],
        out_specs=pl.BlockSpec((tm, d), lambda i: (i, 0)),
        out_shape=jax.ShapeDtypeStruct((s, d), F32),
        scratch_shapes=[pltpu.VMEM((tm + POOL_HALO, d), F32)],
        compiler_params=_params(("arbitrary",), vmem),
        name="pool_mixer",
    )(x, x, _row(g), w, _row(scale))


def _xattn_kernel(x_ref, g_ref, wq_ref, kv_ref, wo_ref, gn_ref, o_ref, hn_ref, *, heads):
    x = x_ref[...]
    h = _rms(x, g_ref[...]).astype(BF16)
    width = wq_ref.shape[1]
    hd = width // heads
    q = (jnp.dot(h, wq_ref[...], preferred_element_type=F32) * (hd ** -0.5)).astype(BF16)
    nt = (((1,), (1,)), ((), ()))
    outs = []
    for a in range(heads):
        cols = slice(a * hd, (a + 1) * hd)
        s = lax.dot_general(q[:, cols], kv_ref[:, cols], nt, preferred_element_type=F32)
        p = jnp.exp(s - jnp.max(s, axis=-1, keepdims=True))
        den = jnp.sum(p, axis=-1, keepdims=True)
        pv = jnp.dot(p.astype(BF16), kv_ref[:, width + a * hd:width + (a + 1) * hd], preferred_element_type=F32)
        outs.append((pv / den).astype(BF16))
    o = jnp.concatenate(outs, axis=-1)
    y = x + jnp.dot(o, wo_ref[...], preferred_element_type=F32)
    o_ref[...] = y
    hn_ref[...] = _rms(y, gn_ref[...]).astype(hn_ref.dtype)


def _xattn(x, g, wq, kv, wo, g_next, li, *, tm_pref=256):
    s, d = x.shape
    width = wq.shape[-1]
    mem_len = kv.shape[0]
    tm = _tile(s, tm_pref, 8)
    vmem = (4 * tm * d * 4 + 2 * tm * d * 2 + 2 * d * width * 2 + mem_len * 2 * width * 2 + tm * d * 6
            + 4 * tm * width * 4)
    layer = lambda shape: pl.BlockSpec((None,) + shape, lambda i: (li, 0, 0), pipeline_mode=pl.Buffered(1))
    rows = pl.BlockSpec((tm, d), lambda i: (i, 0))
    return pl.pallas_call(
        functools.partial(_xattn_kernel, heads=X_HEADS),
        grid=(s // tm,),
        in_specs=[rows, layer((1, d)), layer((d, width)),
                  pl.BlockSpec((mem_len, 2 * width), lambda i: (0, 0), pipeline_mode=pl.Buffered(1)),
                  layer((width, d)), layer((1, d))],
        out_specs=[rows, rows],
        out_shape=[jax.ShapeDtypeStruct((s, d), F32), jax.ShapeDtypeStruct((s, d), BF16)],
        compiler_params=_params(("arbitrary",), vmem),
        name="xattn",
    )(x, _row(g), wq, kv, wo, _row(g_next))


def _ffn_in_kernel(h_ref, wg_ref, wv_ref, cwg_ref, cwv_ref, cbg_ref, cbv_ref, o_ref, cg_ref, cv_ref, *, chunk):
    i, j = pl.program_id(0), pl.program_id(1)
    tm = h_ref.shape[0]

    @pl.when(i == 0)
    def _():
        cg_ref[j] = jnp.zeros(cg_ref.shape[1:], F32)
        cv_ref[j] = jnp.zeros(cv_ref.shape[1:], F32)

    def conv(w_ref, cw_ref, cb_ref, carry_ref):
        u = jnp.dot(h_ref[...], w_ref[...], preferred_element_type=F32)
        first = carry_ref[j]
        carry_ref[j] = u[tm - CONV_HALO:, :]
        out = []
        for r0 in range(0, tm, chunk):
            prev = first if r0 == 0 else u[r0 - CONV_HALO:r0, :]
            ext = jnp.concatenate([prev, u[r0:r0 + chunk, :]], axis=0)
            c = cb_ref[...]
            for tap in range(CONV_WIDTH):
                lag = CONV_WIDTH - 1 - tap
                shifted = pltpu.roll(ext, lag, 0) if lag else ext
                c = c + cw_ref[tap:tap + 1, :] * shifted[CONV_HALO:, :]
            out.append(c)
        return out

    gate = conv(wg_ref, cwg_ref, cbg_ref, cg_ref)
    val = conv(wv_ref, cwv_ref, cbv_ref, cv_ref)
    for k, (gk, vk) in enumerate(zip(gate, val)):
        o_ref[k * chunk:(k + 1) * chunk, :] = (gk / (1.0 + jnp.exp(-gk)) * vk).astype(o_ref.dtype)


def _ffn_in(h, w_in, conv_w, conv_b, li, *, tm_pref=1024, tn_pref=768, chunk_pref=256):
    m, d = h.shape
    f = w_in.shape[-1] // 2
    tm, tn = _tile(m, tm_pref, 8), _tile(f, tn_pref, LANES)
    chunk = _tile(tm, chunk_pref, 8)
    nj = f // tn
    lo = lambda rows: pl.BlockSpec((None, rows, tn), lambda i, j: (li, 0, j))
    hi = lambda rows: pl.BlockSpec((None, rows, tn), lambda i, j: (li, 0, j + nj))
    vmem = 2 * tm * d * 2 + 4 * d * tn * 2 + 2 * tm * tn * 2 + 2 * nj * CONV_HALO * tn * 4 + 6 * tm * tn * 4
    return pl.pallas_call(
        functools.partial(_ffn_in_kernel, chunk=chunk),
        grid=(m // tm, nj),
        in_specs=[pl.BlockSpec((tm, d), lambda i, j: (i, 0)),
                  lo(d), hi(d), lo(CONV_WIDTH), hi(CONV_WIDTH), lo(1), hi(1)],
        out_specs=pl.BlockSpec((tm, tn), lambda i, j: (i, j)),
        out_shape=jax.ShapeDtypeStruct((m, f), BF16),
        scratch_shapes=[pltpu.VMEM((nj, CONV_HALO, tn), F32),
                        pltpu.VMEM((nj, CONV_HALO, tn), F32)],
        compiler_params=_params(("arbitrary", "arbitrary"), vmem),
        name="ffn_in",
    )(h, w_in, w_in, conv_w, conv_w, _row(conv_b), _row(conv_b))


def _final_norm_kernel(x_ref, g_ref, o_ref):
    o_ref[...] = _rms(x_ref[...], g_ref[...])


def _final_norm(x, g, *, tm_pref=512):
    s, d = x.shape
    tm = _tile(s, tm_pref, 8)
    return pl.pallas_call(
        _final_norm_kernel,
        grid=(s // tm,),
        in_specs=[pl.BlockSpec((tm, d), lambda i: (i, 0)), pl.BlockSpec((1, d), lambda i: (0, 0))],
        out_specs=pl.BlockSpec((tm, d), lambda i: (i, 0)),
        out_shape=jax.ShapeDtypeStruct((s, d), F32),
        compiler_params=_params(("arbitrary",), 6 * tm * d * 4),
        name="final_norm",
    )(x, g.reshape(1, d))


def kernel(x, mem, positions, swa_norm, swa_w_qkv, swa_sinks, swa_w_o, diff_norm, diff_w_qkv, diff_lambda_q1, diff_lambda_k1, diff_lambda_q2, diff_lambda_k2, diff_subln, diff_w_o, pool_norm, pool_w, pool_scale, xattn_norm, xattn_mem_norm, xattn_w_q, xattn_w_kv, xattn_w_o, ffn_norm, ffn_w_in, ffn_conv_w, ffn_conv_b, ffn_w_out, final_norm):
    batch, seq, d = x.shape
    assert batch == 1, "the kernels fold no batch dimension"
    depth = ffn_w_in.shape[0]
    xs = x.reshape(seq, d)
    mem2 = mem.reshape(mem.shape[1], d)
    tables = _rope_tables(positions)
    bf = lambda w: w.astype(BF16)
    swa_w_qkv, swa_w_o, diff_w_qkv, diff_w_o, pool_w = map(bf, (swa_w_qkv, swa_w_o, diff_w_qkv, diff_w_o, pool_w))
    xattn_w_q, xattn_w_kv, xattn_w_o, ffn_w_in, ffn_w_out = map(
        bf, (xattn_w_q, xattn_w_kv, xattn_w_o, ffn_w_in, ffn_w_out))

    swa_hq = d // HEAD_DIM
    swa_hkv = (swa_w_qkv.shape[-1] // HEAD_DIM - swa_hq) // 2
    diff_heads = d // (2 * HEAD_DIM)
    diff_hkv = (diff_w_qkv.shape[-1] // (2 * HEAD_DIM) - diff_heads) // 2

    for i in range(depth):
        kind, j = i % N_MIXERS, i // N_MIXERS
        if kind == 0:
            qkv = _qkv_rope(xs, swa_norm, swa_w_qkv, j, tables, n_q=swa_hq * HEAD_DIM, n_k=swa_hkv * HEAD_DIM,
                            q_variant=ROPE_Q_SWA)
            att = _swa_attention(qkv, swa_sinks[j], hq=swa_hq, hkv=swa_hkv)
            xs = _matmul_residual(att, swa_w_o, j, xs, tm_pref=1024, tn_pref=512)
        elif kind == 1:
            lambda_init = 0.8 - 0.6 * math.exp(-0.3 * i)
            qkv = _qkv_rope(xs, diff_norm, diff_w_qkv, j, tables, n_q=2 * diff_heads * HEAD_DIM,
                            n_k=2 * diff_hkv * HEAD_DIM, q_variant=ROPE_Q_DIFF)
            lam_vecs = jnp.stack([diff_lambda_q1[j], diff_lambda_k1[j], diff_lambda_q2[j], diff_lambda_k2[j]])
            att = _diff_attention(qkv, lam_vecs, diff_subln, j, heads=diff_heads, hkv=diff_hkv,
                                  lambda_init=lambda_init)
            xs = _matmul_residual(att, diff_w_o, j, xs, tm_pref=1024, tn_pref=512)
        else:
            xs = _pool_mixer(xs, pool_norm, pool_w, pool_scale, j)
        kv = _norm_matmul(mem2, xattn_mem_norm, xattn_w_kv, i)
        xs, hn = _xattn(xs, xattn_norm, xattn_w_q, kv, xattn_w_o, ffn_norm, i)
        act = _ffn_in(hn, ffn_w_in, ffn_conv_w, ffn_conv_b, i)
        xs = _matmul_residual(act, ffn_w_out, i, xs, tm_pref=512, tn_pref=512)
    return _final_norm(xs, final_norm).reshape(batch, seq, d)
```

```python
import functools
import math

import jax
import jax.numpy as jnp
from jax import lax
from jax.experimental import pallas as pl
from jax.experimental.pallas import tpu as pltpu

HEAD_DIM = 128
ROPE_DIM = HEAD_DIM // 4
ROPE_HALF = ROPE_DIM // 2
ROPE_THETA = 500000.0
SWA_BLOCK = 128
POOL_WINDOWS = (2, 4, 8, 16)
POOL_HALO = 16
X_HEADS = 4
CONV_WIDTH = 3
CONV_HALO = 8
N_MIXERS = 3
NORM_EPS = 1e-5
LANES = 128
LOG2E = math.log2(math.e)

V7X_VMEM_BYTES = 64 * 2**20
VMEM_RESERVE_BYTES = 6 * 2**20

F32 = jnp.float32
BF16 = jnp.bfloat16

ROPE_Q_SWA, ROPE_Q_DIFF, ROPE_K, ROPE_NONE = 0, 1, 2, 3
ROPE_SCALES = (HEAD_DIM ** -0.5, HEAD_DIM ** -0.5 * LOG2E, 1.0)


def _params(semantics, vmem_bytes):
    limit = min(int(vmem_bytes) * 5 // 4 + VMEM_RESERVE_BYTES, V7X_VMEM_BYTES - VMEM_RESERVE_BYTES // 2)
    return pltpu.CompilerParams(dimension_semantics=semantics, vmem_limit_bytes=limit)


def _tile(dim, pref, align):
    if dim <= pref:
        return dim
    t = (pref // align) * align
    while t > align and dim % t:
        t -= align
    assert dim % t == 0, (dim, pref, align)
    return t


def _rms(x, g, eps=NORM_EPS):
    ms = jnp.mean(x * x, axis=-1, keepdims=True)
    return x * lax.rsqrt(ms + eps) * g


def _row(stack):
    return stack.reshape(stack.shape[0], 1, stack.shape[-1])


def _rope_table_kernel(pos_ref, invf_ref, cos_ref, sa_ref, sb_ref):
    ang = pos_ref[...].astype(F32) * invf_ref[...]
    lane = lax.broadcasted_iota(jnp.int32, ang.shape, 1)
    s = jnp.sin(ang)
    cos = jnp.cos(ang)
    sa = jnp.where(lane >= ROPE_HALF, s, 0.0)
    sb = jnp.where(lane < ROPE_HALF, -s, 0.0)
    for v, scale in enumerate(ROPE_SCALES):
        cos_ref[v], sa_ref[v], sb_ref[v] = cos * scale, sa * scale, sb * scale
    cos_ref[ROPE_NONE] = jnp.ones(cos.shape, F32)
    sa_ref[ROPE_NONE] = jnp.zeros(cos.shape, F32)
    sb_ref[ROPE_NONE] = jnp.zeros(cos.shape, F32)


def _rope_tables(positions):
    s = positions.shape[-1]
    inv_freq = ROPE_THETA ** (-jnp.arange(0, ROPE_DIM, 2, dtype=F32) / ROPE_DIM)
    invf = jnp.zeros((1, LANES), F32).at[0, :ROPE_DIM].set(jnp.concatenate([inv_freq, inv_freq]))
    nv = len(ROPE_SCALES) + 1
    tm = _tile(s, 1024, 8)
    tab = jax.ShapeDtypeStruct((nv, s, LANES), F32)
    return pl.pallas_call(
        _rope_table_kernel,
        grid=(s // tm,),
        in_specs=[pl.BlockSpec((tm, 1), lambda i: (i, 0)), pl.BlockSpec((1, LANES), lambda i: (0, 0))],
        out_specs=[pl.BlockSpec((nv, tm, LANES), lambda i: (0, i, 0))] * 3,
        out_shape=[tab] * 3,
        compiler_params=_params(("arbitrary",), (6 * nv + 8) * tm * LANES * 4),
        name="rope_tables",
    )(positions.reshape(s, 1), invf)


def _norm_to_scratch(x_ref, g_ref, h_ref):
    @pl.when(pl.program_id(1) == 0)
    def _():
        h_ref[...] = _rms(x_ref[...], g_ref[...]).astype(h_ref.dtype)


def _norm_matmul_kernel(x_ref, g_ref, w_ref, o_ref, h_ref):
    _norm_to_scratch(x_ref, g_ref, h_ref)
    o_ref[...] = jnp.dot(h_ref[...], w_ref[...], preferred_element_type=F32).astype(o_ref.dtype)


def _norm_matmul(x, g, w, li, *, tm_pref=512, tn_pref=1024):
    m, d = x.shape
    n = w.shape[-1]
    tm, tn = _tile(m, tm_pref, 8), _tile(n, tn_pref, LANES)
    vmem = 2 * tm * d * 4 + tm * d * 2 + 2 * d * tn * 2 + 2 * tm * tn * 2 + 2 * tm * tn * 4
    return pl.pallas_call(
        _norm_matmul_kernel,
        grid=(m // tm, n // tn),
        in_specs=[pl.BlockSpec((tm, d), lambda i, j: (i, 0)),
                  pl.BlockSpec((None, 1, d), lambda i, j: (li, 0, 0)),
                  pl.BlockSpec((None, d, tn), lambda i, j: (li, 0, j))],
        out_specs=pl.BlockSpec((tm, tn), lambda i, j: (i, j)),
        out_shape=jax.ShapeDtypeStruct((m, n), BF16),
        scratch_shapes=[pltpu.VMEM((tm, d), BF16)],
        compiler_params=_params(("arbitrary", "arbitrary"), vmem),
        name="norm_matmul",
    )(x, _row(g), w)


def _qkv_rope_kernel(x_ref, g_ref, w_ref, cos_ref, sa_ref, sb_ref, o_ref, h_ref, *, tail_rows):
    _norm_to_scratch(x_ref, g_ref, h_ref)
    tm = h_ref.shape[0]
    cut = tm - tail_rows if 0 < tail_rows < tm else 0
    for rows in ((slice(0, cut), slice(cut, tm)) if cut else (slice(0, tm),)):
        y = jnp.dot(h_ref[rows, :], w_ref[...], preferred_element_type=F32)
        cos, sa, sb = cos_ref[rows, :], sa_ref[rows, :], sb_ref[rows, :]
        for c in range(y.shape[1] // HEAD_DIM):
            sl = slice(c * HEAD_DIM, (c + 1) * HEAD_DIM)
            t = y[:, sl]
            r = t * cos + pltpu.roll(t, ROPE_HALF, 1) * sa + pltpu.roll(t, HEAD_DIM - ROPE_HALF, 1) * sb
            o_ref[rows, sl] = r.astype(o_ref.dtype)


def _qkv_rope(x, g, w, li, tables, *, n_q, n_k, q_variant, tm_pref=512):
    m, d = x.shape
    n = w.shape[-1]
    tn = _tile(math.gcd(n_q, n_k, n - n_q - n_k), 1024, HEAD_DIM)
    tm = _tile(m, tm_pref, 8)
    nqb, nkb = n_q // tn, n_k // tn

    def tab_index(i, j):
        return (jnp.where(j < nqb, q_variant, jnp.where(j < nqb + nkb, ROPE_K, ROPE_NONE)), i, 0)

    tab_spec = pl.BlockSpec((None, tm, LANES), tab_index)
    vmem = 2 * tm * d * 4 + tm * d * 2 + 2 * d * tn * 2 + 2 * tm * tn * 2 + 3 * tm * tn * 4
    return pl.pallas_call(
        functools.partial(_qkv_rope_kernel, tail_rows=LANES),
        grid=(m // tm, n // tn),
        in_specs=[pl.BlockSpec((tm, d), lambda i, j: (i, 0)),
                  pl.BlockSpec((None, 1, d), lambda i, j: (li, 0, 0)),
                  pl.BlockSpec((None, d, tn), lambda i, j: (li, 0, j)),
                  tab_spec, tab_spec, tab_spec],
        out_specs=pl.BlockSpec((tm, tn), lambda i, j: (i, j)),
        out_shape=jax.ShapeDtypeStruct((m, n), BF16),
        scratch_shapes=[pltpu.VMEM((tm, d), BF16)],
        compiler_params=_params(("arbitrary", "arbitrary"), vmem),
        name="qkv_rope",
    )(x, _row(g), w, *tables)


def _matmul_residual_kernel(a_ref, w_ref, x_ref, o_ref):
    o_ref[...] = x_ref[...] + jnp.dot(a_ref[...], w_ref[...], preferred_element_type=F32)


def _matmul_residual(a, w, li, x, *, tm_pref, tn_pref):
    m, kd = a.shape
    n = w.shape[-1]
    tm, tn = _tile(m, tm_pref, 8), _tile(n, tn_pref, LANES)
    vmem = 2 * tm * kd * 2 + 2 * kd * tn * 2 + 5 * tm * tn * 4
    return pl.pallas_call(
        _matmul_residual_kernel,
        grid=(m // tm, n // tn),
        in_specs=[pl.BlockSpec((tm, kd), lambda i, j: (i, 0)),
                  pl.BlockSpec((None, kd, tn), lambda i, j: (li, 0, j)),
                  pl.BlockSpec((tm, tn), lambda i, j: (i, j))],
        out_specs=pl.BlockSpec((tm, tn), lambda i, j: (i, j)),
        out_shape=jax.ShapeDtypeStruct((m, n), F32),
        compiler_params=_params(("arbitrary", "arbitrary"), vmem),
        name="matmul_residual",
    )(a, w, x)


def _swa_kernel(sink_ref, q_ref, kc_ref, kp_ref, vc_ref, vp_ref, o_ref, *, group, hps):
    i, hb = pl.program_id(0), pl.program_id(1)
    blk = SWA_BLOCK
    nsub = q_ref.shape[0] // blk
    row = lax.broadcasted_iota(jnp.int32, (blk, 2 * blk), 0)
    col = lax.broadcasted_iota(jnp.int32, (blk, 2 * blk), 1)
    rel = col - blk
    band = (rel <= row) & (row - rel < blk)
    nt = (((1,), (1,)), ((), ()))
    for c in range(nsub):
        rows = slice(c * blk, (c + 1) * blk)
        if c == 0:
            mask = band & (rel >= jnp.where(i > 0, -blk, 0))
        else:
            prev = slice((c - 1) * blk, c * blk)
            mask = band
        for hh in range(hps):
            kcols = slice(hh * HEAD_DIM, (hh + 1) * HEAD_DIM)
            if c == 0:
                kprev, vprev = kp_ref[:, kcols], vp_ref[:, kcols]
            else:
                kprev, vprev = kc_ref[prev, kcols], vc_ref[prev, kcols]
            kw = jnp.concatenate([kprev, kc_ref[rows, kcols]], axis=0)
            vw = jnp.concatenate([vprev, vc_ref[rows, kcols]], axis=0)
            qcol = lambda g: slice((hh * group + g) * HEAD_DIM, (hh * group + g + 1) * HEAD_DIM)
            qs = jnp.concatenate([q_ref[rows, qcol(g)] for g in range(group)], axis=0)
            s = lax.dot_general(qs, kw, nt, preferred_element_type=F32)
            probs, inv = [], []
            for g in range(group):
                sg = jnp.where(mask, s[g * blk:(g + 1) * blk], -jnp.inf)
                sink = sink_ref[(hb * hps + hh) * group + g]
                mx = jnp.maximum(jnp.max(sg, axis=-1, keepdims=True), sink)
                p = jnp.exp(sg - mx)
                inv.append(1.0 / (jnp.sum(p, axis=-1, keepdims=True) + jnp.exp(sink - mx)))
                probs.append(p.astype(BF16))
            o = jnp.dot(jnp.concatenate(probs, axis=0), vw, preferred_element_type=F32)
            for g in range(group):
                o_ref[rows, qcol(g)] = (o[g * blk:(g + 1) * blk] * inv[g]).astype(o_ref.dtype)


def _swa_attention(qkv, sinks, *, hq, hkv, tq_pref=512):
    s = qkv.shape[0]
    group = hq // hkv
    hps = next(c for c in (4, 2, 1) if hkv % c == 0)
    tq = _tile(s, tq_pref, SWA_BLOCK)
    r = tq // SWA_BLOCK
    kw = hps * HEAD_DIM
    cur = lambda off: pl.BlockSpec((tq, kw), lambda i, h, sk: (i, off // hps + h))
    prev = lambda off: pl.BlockSpec((SWA_BLOCK, kw), lambda i, h, sk: (jnp.maximum(i * r - 1, 0), off // hps + h))
    vmem = (4 * tq * hps * group * HEAD_DIM * 2 + 8 * tq * kw * 2
            + 8 * hps * group * SWA_BLOCK * 2 * SWA_BLOCK * 4)
    return pl.pallas_call(
        functools.partial(_swa_kernel, group=group, hps=hps),
        grid_spec=pltpu.PrefetchScalarGridSpec(
            num_scalar_prefetch=1,
            grid=(s // tq, hkv // hps),
            in_specs=[pl.BlockSpec((tq, hps * group * HEAD_DIM), lambda i, h, sk: (i, h)),
                      cur(hq), prev(hq), cur(hq + hkv), prev(hq + hkv)],
            out_specs=pl.BlockSpec((tq, hps * group * HEAD_DIM), lambda i, h, sk: (i, h)),
        ),
        out_shape=jax.ShapeDtypeStruct((s, hq * HEAD_DIM), BF16),
        compiler_params=_params(("arbitrary", "arbitrary"), vmem),
        name="swa_attention",
    )(sinks, qkv, qkv, qkv, qkv, qkv)


def _diff_kernel(lam_ref, subln_ref, q_ref, k_ref, v_ref, o_ref, qs_ref, m_ref, l_ref, acc_ref, sa_ref, sb_ref,
                 *, group, tk, lambda_init):
    qi = pl.program_id(1)
    tq = q_ref.shape[0]
    rows = group * tq
    nt = (((1,), (1,)), ((), ()))

    for mp in range(2):
        for g in range(group):
            c = (g * 2 + mp) * HEAD_DIM
            qs_ref[mp, g * tq:(g + 1) * tq, :] = q_ref[:, c:c + HEAD_DIM]
    m_ref[...] = jnp.full(m_ref.shape, -jnp.inf, F32)
    l_ref[...] = jnp.zeros(l_ref.shape, F32)
    acc_ref[...] = jnp.zeros(acc_ref.shape, F32)

    def scores(kb, dst_ref):
        st = pl.multiple_of(kb * tk, tk)
        for mp in range(2):
            dst_ref[mp] = lax.dot_general(qs_ref[mp], k_ref[pl.ds(st, tk), mp * HEAD_DIM:(mp + 1) * HEAD_DIM], nt,
                                          preferred_element_type=F32)

    def update(kb, src_ref, nkeys, masked):
        st = pl.multiple_of(kb * tk, tk)
        vblk = v_ref[pl.ds(st, nkeys), :]
        if masked:
            qpos = qi * tq + (lax.broadcasted_iota(jnp.int32, (rows, nkeys), 0) & (tq - 1))
            kpos = st + lax.broadcasted_iota(jnp.int32, (rows, nkeys), 1)
            visible = kpos <= qpos
        for mp in range(2):
            s = src_ref[mp, :, :nkeys]
            if masked:
                s = jnp.where(visible, s, -jnp.inf)
            m_prev = m_ref[mp]
            m_next = jnp.maximum(m_prev, jnp.max(s, axis=-1, keepdims=True))
            alpha = jnp.exp2(m_prev - m_next)
            p = jnp.exp2(s - pltpu.repeat(m_next, nkeys // LANES, 1))
            lane_sum = p[:, :LANES]
            for t in range(1, nkeys // LANES):
                lane_sum = lane_sum + p[:, t * LANES:(t + 1) * LANES]
            l_ref[mp] = alpha * l_ref[mp] + lane_sum
            pv = jnp.dot(p.astype(BF16), vblk, preferred_element_type=F32)
            acc_ref[mp] = acc_ref[mp] * pltpu.repeat(alpha, acc_ref.shape[2] // LANES, 1) + pv
            m_ref[mp] = m_next

    n_full = (qi * tq) // tk
    scores(0, sa_ref)

    def body(t, carry):
        @pl.when(t % 2 == 0)
        def _():
            scores(t + 1, sb_ref)
            update(t, sa_ref, tk, False)

        @pl.when(t % 2 == 1)
        def _():
            scores(t + 1, sa_ref)
            update(t, sb_ref, tk, False)

        return carry

    lax.fori_loop(0, n_full, body, 0)

    sub = qi % (tk // tq)
    for k in range(tk // tq):
        for parity, buf in enumerate((sa_ref, sb_ref)):
            @pl.when((sub == k) & (n_full % 2 == parity))
            def _():
                update(n_full, buf, (k + 1) * tq, True)

    lam_vecs = lam_ref[...]
    lam = (jnp.exp(jnp.sum(lam_vecs[0:1] * lam_vecs[1:2], axis=-1, keepdims=True))
           - jnp.exp(jnp.sum(lam_vecs[2:3] * lam_vecs[3:4], axis=-1, keepdims=True)) + lambda_init)
    o = (acc_ref[0] / jnp.sum(l_ref[0], axis=-1, keepdims=True)
         - lam * (acc_ref[1] / jnp.sum(l_ref[1], axis=-1, keepdims=True)))
    y = _rms(o, subln_ref[...]) * (1.0 - lambda_init)
    vd = y.shape[1]
    for g in range(group):
        o_ref[:, g * vd:(g + 1) * vd] = y[g * tq:(g + 1) * tq].astype(o_ref.dtype)


def _diff_attention(qkv, lam_vecs, subln, li, *, heads, hkv, lambda_init, tq_pref=256, tk_pref=1024):
    s = qkv.shape[0]
    group = heads // hkv
    vd = 2 * HEAD_DIM
    tq = _tile(s, tq_pref, 8)
    tk = _tile(s, tk_pref, LANES)
    assert tk % tq == 0 and tq & (tq - 1) == 0, (tq, tk)
    rows = group * tq
    vmem = (2 * s * vd * 2 + 4 * tq * group * vd * 2 + 2 * rows * HEAD_DIM * 2
            + 4 * rows * LANES * 4 + 2 * rows * vd * 4 + 4 * rows * tk * 4 + 2 * rows * tk * 4)
    kern = functools.partial(_diff_kernel, group=group, tk=tk, lambda_init=lambda_init)
    resident = lambda off: pl.BlockSpec((s, vd), lambda h, i: (0, off + h), pipeline_mode=pl.Buffered(1))
    return pl.pallas_call(
        kern,
        grid=(hkv, s // tq),
        in_specs=[pl.BlockSpec((4, HEAD_DIM), lambda h, i: (0, 0)),
                  pl.BlockSpec((None, 1, vd), lambda h, i: (li, 0, 0)),
                  pl.BlockSpec((tq, group * vd), lambda h, i: (i, h)),
                  resident(heads), resident(heads + hkv)],
        out_specs=pl.BlockSpec((tq, group * vd), lambda h, i: (i, h)),
        out_shape=jax.ShapeDtypeStruct((s, heads * vd), BF16),
        scratch_shapes=[pltpu.VMEM((2, rows, HEAD_DIM), BF16),
                        pltpu.VMEM((2, rows, LANES), F32),
                        pltpu.VMEM((2, rows, LANES), F32),
                        pltpu.VMEM((2, rows, vd), F32),
                        pltpu.VMEM((2, rows, tk), F32),
                        pltpu.VMEM((2, rows, tk), F32)],
        compiler_params=_params(("arbitrary", "arbitrary"), vmem),
        name="diff_attention",
    )(lam_vecs, _row(subln), qkv, qkv, qkv)


def _pool_kernel(x_ref, xp_ref, g_ref, w_ref, scale_ref, o_ref, ext_ref):
    i = pl.program_id(0)
    tm = x_ref.shape[0]
    ch = w_ref.shape[1]
    x = x_ref[...]
    h = _rms(x, g_ref[...])
    hp = _rms(xp_ref[...], g_ref[...])
    ext_ref[0:POOL_HALO, :] = jnp.where(i > 0, hp, 0.0)
    ext_ref[POOL_HALO:, :] = h
    t = i * tm + lax.broadcasted_iota(jnp.int32, (tm, 1), 0)
    for gi, win in enumerate(POOL_WINDOWS):
        cols = slice(gi * ch, (gi + 1) * ch)
        e = ext_ref[:, cols]
        span = 1
        while span < win:
            e = e + pltpu.roll(e, span, 0)
            span *= 2
        count = jnp.minimum(t + 1, win).astype(F32)
        pooled = (e[POOL_HALO:] / count - h[:, cols]).astype(BF16)
        y = jnp.dot(pooled, w_ref[gi], preferred_element_type=F32)
        o_ref[:, cols] = x[:, cols] + y * scale_ref[:, cols]


def _pool_mixer(x, g, w, scale, li, *, tm_pref=256):
    s, d = x.shape
    groups, ch = w.shape[1], w.shape[2]
    assert groups == len(POOL_WINDOWS)
    tm = _tile(s, tm_pref, POOL_HALO)
    r = tm // POOL_HALO
    vec = pl.BlockSpec((None, 1, d), lambda i: (li, 0, 0))
    vmem = 4 * tm * d * 4 + 2 * groups * ch * ch * 2 + (tm + POOL_HALO) * d * 4 + 6 * tm * ch * 4
    return pl.pallas_call(
        _pool_kernel,
        grid=(s // tm,),
        in_specs=[pl.BlockSpec((tm, d), lambda i: (i, 0)),
                  pl.BlockSpec((POOL_HALO, d), lambda i: (jnp.maximum(i * r - 1, 0), 0)),
                  vec,
                  pl.BlockSpec((None, groups, ch, ch), lambda i: (li, 0, 0, 0)),
                  vec],
        out_specs=pl.BlockSpec((tm, d), lambda i: (i, 0)),
        out_shape=jax.ShapeDtypeStruct((s, d), F32),
        scratch_shapes=[pltpu.VMEM((tm + POOL_HALO, d), F32)],
        compiler_params=_params(("arbitrary",), vmem),
        name="pool_mixer",
    )(x, x, _row(g), w, _row(scale))


def _xattn_kernel(x_ref, g_ref, wq_ref, kv_ref, wo_ref, gn_ref, o_ref, hn_ref, *, heads):
    x = x_ref[...]
    h = _rms(x, g_ref[...]).astype(BF16)
    width = wq_ref.shape[1]
    hd = width // heads
    q = (jnp.dot(h, wq_ref[...], preferred_element_type=F32) * (hd ** -0.5)).astype(BF16)
    nt = (((1,), (1,)), ((), ()))
    outs = []
    for a in range(heads):
        cols = slice(a * hd, (a + 1) * hd)
        s = lax.dot_general(q[:, cols], kv_ref[:, cols], nt, preferred_element_type=F32)
        p = jnp.exp(s - jnp.max(s, axis=-1, keepdims=True))
        den = jnp.sum(p, axis=-1, keepdims=True)
        pv = jnp.dot(p.astype(BF16), kv_ref[:, width + a * hd:width + (a + 1) * hd], preferred_element_type=F32)
        outs.append((pv / den).astype(BF16))
    o = jnp.concatenate(outs, axis=-1)
    y = x + jnp.dot(o, wo_ref[...], preferred_element_type=F32)
    o_ref[...] = y
    hn_ref[...] = _rms(y, gn_ref[...]).astype(hn_ref.dtype)


def _xattn(x, g, wq, kv, wo, g_next, li, *, tm_pref=256):
    s, d = x.shape
    width = wq.shape[-1]
    mem_len = kv.shape[0]
    tm = _tile(s, tm_pref, 8)
    vmem = (4 * tm * d * 4 + 2 * tm * d * 2 + 2 * d * width * 2 + mem_len * 2 * width * 2 + tm * d * 6
            + 4 * tm * width * 4)
    layer = lambda shape: pl.BlockSpec((None,) + shape, lambda i: (li, 0, 0), pipeline_mode=pl.Buffered(1))
    rows = pl.BlockSpec((tm, d), lambda i: (i, 0))
    return pl.pallas_call(
        functools.partial(_xattn_kernel, heads=X_HEADS),
        grid=(s // tm,),
        in_specs=[rows, layer((1, d)), layer((d, width)),
                  pl.BlockSpec((mem_len, 2 * width), lambda i: (0, 0), pipeline_mode=pl.Buffered(1)),
                  layer((width, d)), layer((1, d))],
        out_specs=[rows, rows],
        out_shape=[jax.ShapeDtypeStruct((s, d), F32), jax.ShapeDtypeStruct((s, d), BF16)],
        compiler_params=_params(("arbitrary",), vmem),
        name="xattn",
    )(x, _row(g), wq, kv, wo, _row(g_next))


def _ffn_in_kernel(h_ref, wg_ref, wv_ref, cwg_ref, cwv_ref, cbg_ref, cbv_ref, o_ref, cg_ref, cv_ref, *, chunk):
    i, j = pl.program_id(0), pl.program_id(1)
    tm = h_ref.shape[0]

    @pl.when(i == 0)
    def _():
        cg_ref[j] = jnp.zeros(cg_ref.shape[1:], F32)
        cv_ref[j] = jnp.zeros(cv_ref.shape[1:], F32)

    def conv(w_ref, cw_ref, cb_ref, carry_ref, tail_rows=0):
        if tail_rows:
            cut = tm - tail_rows
            u = jnp.concatenate([jnp.dot(h_ref[:cut, :], w_ref[...], preferred_element_type=F32),
                                 jnp.dot(h_ref[cut:, :], w_ref[...], preferred_element_type=F32)], axis=0)
        else:
            u = jnp.dot(h_ref[...], w_ref[...], preferred_element_type=F32)
        first = carry_ref[j]
        carry_ref[j] = u[tm - CONV_HALO:, :]
        out = []
        for r0 in range(0, tm, chunk):
            prev = first if r0 == 0 else u[r0 - CONV_HALO:r0, :]
            ext = jnp.concatenate([prev, u[r0:r0 + chunk, :]], axis=0)
            c = cb_ref[...]
            for tap in range(CONV_WIDTH):
                lag = CONV_WIDTH - 1 - tap
                shifted = pltpu.roll(ext, lag, 0) if lag else ext
                c = c + cw_ref[tap:tap + 1, :] * shifted[CONV_HALO:, :]
            out.append(c)
        return out

    gate = conv(wg_ref, cwg_ref, cbg_ref, cg_ref)
    val = conv(wv_ref, cwv_ref, cbv_ref, cv_ref, tail_rows=chunk)
    for k, (gk, vk) in enumerate(zip(gate, val)):
        o_ref[k * chunk:(k + 1) * chunk, :] = (gk / (1.0 + jnp.exp(-gk)) * vk).astype(o_ref.dtype)


def _ffn_in(h, w_in, conv_w, conv_b, li, *, tm_pref=1024, tn_pref=768, chunk_pref=256):
    m, d = h.shape
    f = w_in.shape[-1] // 2
    tm, tn = _tile(m, tm_pref, 8), _tile(f, tn_pref, LANES)
    chunk = _tile(tm, chunk_pref, 8)
    nj = f // tn
    lo = lambda rows: pl.BlockSpec((None, rows, tn), lambda i, j: (li, 0, j))
    hi = lambda rows: pl.BlockSpec((None, rows, tn), lambda i, j: (li, 0, j + nj))
    vmem = 2 * tm * d * 2 + 4 * d * tn * 2 + 2 * tm * tn * 2 + 2 * nj * CONV_HALO * tn * 4 + 6 * tm * tn * 4
    return pl.pallas_call(
        functools.partial(_ffn_in_kernel, chunk=chunk),
        grid=(m // tm, nj),
        in_specs=[pl.BlockSpec((tm, d), lambda i, j: (i, 0)),
                  lo(d), hi(d), lo(CONV_WIDTH), hi(CONV_WIDTH), lo(1), hi(1)],
        out_specs=pl.BlockSpec((tm, tn), lambda i, j: (i, j)),
        out_shape=jax.ShapeDtypeStruct((m, f), BF16),
        scratch_shapes=[pltpu.VMEM((nj, CONV_HALO, tn), F32),
                        pltpu.VMEM((nj, CONV_HALO, tn), F32)],
        compiler_params=_params(("arbitrary", "arbitrary"), vmem),
        name="ffn_in",
    )(h, w_in, w_in, conv_w, conv_w, _row(conv_b), _row(conv_b))


def _final_norm_kernel(x_ref, g_ref, o_ref):
    o_ref[...] = _rms(x_ref[...], g_ref[...])


def _final_norm(x, g, *, tm_pref=512):
    s, d = x.shape
    tm = _tile(s, tm_pref, 8)
    return pl.pallas_call(
        _final_norm_kernel,
        grid=(s // tm,),
        in_specs=[pl.BlockSpec((tm, d), lambda i: (i, 0)), pl.BlockSpec((1, d), lambda i: (0, 0))],
        out_specs=pl.BlockSpec((tm, d), lambda i: (i, 0)),
        out_shape=jax.ShapeDtypeStruct((s, d), F32),
        compiler_params=_params(("arbitrary",), 6 * tm * d * 4),
        name="final_norm",
    )(x, g.reshape(1, d))


def kernel(x, mem, positions, swa_norm, swa_w_qkv, swa_sinks, swa_w_o, diff_norm, diff_w_qkv, diff_lambda_q1, diff_lambda_k1, diff_lambda_q2, diff_lambda_k2, diff_subln, diff_w_o, pool_norm, pool_w, pool_scale, xattn_norm, xattn_mem_norm, xattn_w_q, xattn_w_kv, xattn_w_o, ffn_norm, ffn_w_in, ffn_conv_w, ffn_conv_b, ffn_w_out, final_norm):
    batch, seq, d = x.shape
    assert batch == 1, "the kernels fold no batch dimension"
    depth = ffn_w_in.shape[0]
    xs = x.reshape(seq, d)
    mem2 = mem.reshape(mem.shape[1], d)
    tables = _rope_tables(positions)
    bf = lambda w: w.astype(BF16)
    swa_w_qkv, swa_w_o, diff_w_qkv, diff_w_o, pool_w = map(bf, (swa_w_qkv, swa_w_o, diff_w_qkv, diff_w_o, pool_w))
    xattn_w_q, xattn_w_kv, xattn_w_o, ffn_w_in, ffn_w_out = map(
        bf, (xattn_w_q, xattn_w_kv, xattn_w_o, ffn_w_in, ffn_w_out))

    swa_hq = d // HEAD_DIM
    swa_hkv = (swa_w_qkv.shape[-1] // HEAD_DIM - swa_hq) // 2
    diff_heads = d // (2 * HEAD_DIM)
    diff_hkv = (diff_w_qkv.shape[-1] // (2 * HEAD_DIM) - diff_heads) // 2

    for i in range(depth):
        kind, j = i % N_MIXERS, i // N_MIXERS
        if kind == 0:
            qkv = _qkv_rope(xs, swa_norm, swa_w_qkv, j, tables, n_q=swa_hq * HEAD_DIM, n_k=swa_hkv * HEAD_DIM,
                            q_variant=ROPE_Q_SWA)
            att = _swa_attention(qkv, swa_sinks[j], hq=swa_hq, hkv=swa_hkv)
            xs = _matmul_residual(att, swa_w_o, j, xs, tm_pref=1024, tn_pref=512)
        elif kind == 1:
            lambda_init = 0.8 - 0.6 * math.exp(-0.3 * i)
            qkv = _qkv_rope(xs, diff_norm, diff_w_qkv, j, tables, n_q=2 * diff_heads * HEAD_DIM,
                            n_k=2 * diff_hkv * HEAD_DIM, q_variant=ROPE_Q_DIFF)
            lam_vecs = jnp.stack([diff_lambda_q1[j], diff_lambda_k1[j], diff_lambda_q2[j], diff_lambda_k2[j]])
            att = _diff_attention(qkv, lam_vecs, diff_subln, j, heads=diff_heads, hkv=diff_hkv,
                                  lambda_init=lambda_init)
            xs = _matmul_residual(att, diff_w_o, j, xs, tm_pref=1024, tn_pref=512)
        else:
            xs = _pool_mixer(xs, pool_norm, pool_w, pool_scale, j)
        kv = _norm_matmul(mem2, xattn_mem_norm, xattn_w_kv, i)
        xs, hn = _xattn(xs, xattn_norm, xattn_w_q, kv, xattn_w_o, ffn_norm, i)
        act = _ffn_in(hn, ffn_w_in, ffn_conv_w, ffn_conv_b, i)
        xs = _matmul_residual(act, ffn_w_out, i, xs, tm_pref=512, tn_pref=512)
    return _final_norm(xs, final_norm).reshape(batch, seq, d)
```

```python
import functools
import math

import jax
import jax.numpy as jnp
from jax import lax
from jax.experimental import pallas as pl
from jax.experimental.pallas import tpu as pltpu

HEAD_DIM = 128
ROPE_DIM = HEAD_DIM // 4
ROPE_HALF = ROPE_DIM // 2
ROPE_THETA = 500000.0
SWA_BLOCK = 128
POOL_WINDOWS = (2, 4, 8, 16)
POOL_HALO = 16
X_HEADS = 4
CONV_WIDTH = 3
CONV_HALO = 8
N_MIXERS = 3
NORM_EPS = 1e-5
LANES = 128
LOG2E = math.log2(math.e)

V7X_VMEM_BYTES = 64 * 2**20
VMEM_RESERVE_BYTES = 6 * 2**20

F32 = jnp.float32
BF16 = jnp.bfloat16

ROPE_Q_SWA, ROPE_Q_DIFF, ROPE_K, ROPE_NONE = 0, 1, 2, 3
ROPE_SCALES = (HEAD_DIM ** -0.5, HEAD_DIM ** -0.5 * LOG2E, 1.0)


def _params(semantics, vmem_bytes):
    limit = min(int(vmem_bytes) * 5 // 4 + VMEM_RESERVE_BYTES, V7X_VMEM_BYTES - VMEM_RESERVE_BYTES // 2)
    return pltpu.CompilerParams(dimension_semantics=semantics, vmem_limit_bytes=limit)


def _tile(dim, pref, align):
    if dim <= pref:
        return dim
    t = (pref // align) * align
    while t > align and dim % t:
        t -= align
    assert dim % t == 0, (dim, pref, align)
    return t


def _rms(x, g, eps=NORM_EPS):
    ms = jnp.mean(x * x, axis=-1, keepdims=True)
    return x * lax.rsqrt(ms + eps) * g


def _row(stack):
    return stack.reshape(stack.shape[0], 1, stack.shape[-1])


def _rope_table_kernel(pos_ref, invf_ref, cos_ref, sa_ref, sb_ref):
    ang = pos_ref[...].astype(F32) * invf_ref[...]
    lane = lax.broadcasted_iota(jnp.int32, ang.shape, 1)
    s = jnp.sin(ang)
    cos = jnp.cos(ang)
    sa = jnp.where(lane >= ROPE_HALF, s, 0.0)
    sb = jnp.where(lane < ROPE_HALF, -s, 0.0)
    for v, scale in enumerate(ROPE_SCALES):
        cos_ref[v], sa_ref[v], sb_ref[v] = cos * scale, sa * scale, sb * scale
    cos_ref[ROPE_NONE] = jnp.ones(cos.shape, F32)
    sa_ref[ROPE_NONE] = jnp.zeros(cos.shape, F32)
    sb_ref[ROPE_NONE] = jnp.zeros(cos.shape, F32)


def _rope_tables(positions):
    s = positions.shape[-1]
    inv_freq = ROPE_THETA ** (-jnp.arange(0, ROPE_DIM, 2, dtype=F32) / ROPE_DIM)
    invf = jnp.zeros((1, LANES), F32).at[0, :ROPE_DIM].set(jnp.concatenate([inv_freq, inv_freq]))
    nv = len(ROPE_SCALES) + 1
    tm = _tile(s, 1024, 8)
    tab = jax.ShapeDtypeStruct((nv, s, LANES), F32)
    return pl.pallas_call(
        _rope_table_kernel,
        grid=(s // tm,),
        in_specs=[pl.BlockSpec((tm, 1), lambda i: (i, 0)), pl.BlockSpec((1, LANES), lambda i: (0, 0))],
        out_specs=[pl.BlockSpec((nv, tm, LANES), lambda i: (0, i, 0))] * 3,
        out_shape=[tab] * 3,
        compiler_params=_params(("arbitrary",), (6 * nv + 8) * tm * LANES * 4),
        name="rope_tables",
    )(positions.reshape(s, 1), invf)


def _norm_to_scratch(x_ref, g_ref, h_ref):
    @pl.when(pl.program_id(1) == 0)
    def _():
        h_ref[...] = _rms(x_ref[...], g_ref[...]).astype(h_ref.dtype)


def _norm_matmul_kernel(x_ref, g_ref, w_ref, o_ref, h_ref):
    _norm_to_scratch(x_ref, g_ref, h_ref)
    o_ref[...] = jnp.dot(h_ref[...], w_ref[...], preferred_element_type=F32).astype(o_ref.dtype)


def _norm_matmul(x, g, w, li, *, tm_pref=512, tn_pref=1024):
    m, d = x.shape
    n = w.shape[-1]
    tm, tn = _tile(m, tm_pref, 8), _tile(n, tn_pref, LANES)
    vmem = 2 * tm * d * 4 + tm * d * 2 + 2 * d * tn * 2 + 2 * tm * tn * 2 + 2 * tm * tn * 4
    return pl.pallas_call(
        _norm_matmul_kernel,
        grid=(m // tm, n // tn),
        in_specs=[pl.BlockSpec((tm, d), lambda i, j: (i, 0)),
                  pl.BlockSpec((None, 1, d), lambda i, j: (li, 0, 0)),
                  pl.BlockSpec((None, d, tn), lambda i, j: (li, 0, j))],
        out_specs=pl.BlockSpec((tm, tn), lambda i, j: (i, j)),
        out_shape=jax.ShapeDtypeStruct((m, n), BF16),
        scratch_shapes=[pltpu.VMEM((tm, d), BF16)],
        compiler_params=_params(("arbitrary", "arbitrary"), vmem),
        name="norm_matmul",
    )(x, _row(g), w)


def _qkv_rope_kernel(x_ref, g_ref, w_ref, cos_ref, sa_ref, sb_ref, o_ref, h_ref, *, tail_rows):
    _norm_to_scratch(x_ref, g_ref, h_ref)
    tm = h_ref.shape[0]
    cut = tm - tail_rows if 0 < tail_rows < tm else 0
    for rows in ((slice(0, cut), slice(cut, tm)) if cut else (slice(0, tm),)):
        y = jnp.dot(h_ref[rows, :], w_ref[...], preferred_element_type=F32)
        cos, sa, sb = cos_ref[rows, :], sa_ref[rows, :], sb_ref[rows, :]
        for c in range(y.shape[1] // HEAD_DIM):
            sl = slice(c * HEAD_DIM, (c + 1) * HEAD_DIM)
            t = y[:, sl]
            r = t * cos + pltpu.roll(t, ROPE_HALF, 1) * sa + pltpu.roll(t, HEAD_DIM - ROPE_HALF, 1) * sb
            o_ref[rows, sl] = r.astype(o_ref.dtype)


def _qkv_rope(x, g, w, li, tables, *, n_q, n_k, q_variant, tm_pref=512):
    m, d = x.shape
    n = w.shape[-1]
    tn = _tile(math.gcd(n_q, n_k, n - n_q - n_k), 1024, HEAD_DIM)
    tm = _tile(m, tm_pref, 8)
    nqb, nkb = n_q // tn, n_k // tn

    def tab_index(i, j):
        return (jnp.where(j < nqb, q_variant, jnp.where(j < nqb + nkb, ROPE_K, ROPE_NONE)), i, 0)

    tab_spec = pl.BlockSpec((None, tm, LANES), tab_index)
    vmem = 2 * tm * d * 4 + tm * d * 2 + 2 * d * tn * 2 + 2 * tm * tn * 2 + 3 * tm * tn * 4
    return pl.pallas_call(
        functools.partial(_qkv_rope_kernel, tail_rows=LANES),
        grid=(m // tm, n // tn),
        in_specs=[pl.BlockSpec((tm, d), lambda i, j: (i, 0)),
                  pl.BlockSpec((None, 1, d), lambda i, j: (li, 0, 0)),
                  pl.BlockSpec((None, d, tn), lambda i, j: (li, 0, j)),
                  tab_spec, tab_spec, tab_spec],
        out_specs=pl.BlockSpec((tm, tn), lambda i, j: (i, j)),
        out_shape=jax.ShapeDtypeStruct((m, n), BF16),
        scratch_shapes=[pltpu.VMEM((tm, d), BF16)],
        compiler_params=_params(("arbitrary", "arbitrary"), vmem),
        name="qkv_rope",
    )(x, _row(g), w, *tables)


def _matmul_residual_kernel(a_ref, w_ref, x_ref, o_ref):
    o_ref[...] = x_ref[...] + jnp.dot(a_ref[...], w_ref[...], preferred_element_type=F32)


def _matmul_residual_ring_kernel(a_hbm, w_ref, x_ref, o_ref, abuf, sem, *, tm, ni):
    i, j = pl.program_id(0), pl.program_id(1)
    slot = i % 2

    def rows_copy(blk, dst_slot):
        src = a_hbm.at[pl.ds(pl.multiple_of(blk * tm, tm), tm), :]
        return pltpu.make_async_copy(src, abuf.at[dst_slot], sem.at[dst_slot])

    @pl.when(j == 0)
    def _():
        @pl.when(i == 0)
        def _():
            rows_copy(0, 0).start()

        @pl.when(i + 1 < ni)
        def _():
            rows_copy(i + 1, 1 - slot).start()

        rows_copy(i, slot).wait()

    o_ref[...] = x_ref[...] + jnp.dot(abuf[slot], w_ref[...], preferred_element_type=F32)


def _matmul_residual(a, w, li, x, *, tm_pref, tn_pref, ring=False):
    m, kd = a.shape
    n = w.shape[-1]
    tm, tn = _tile(m, tm_pref, 8), _tile(n, tn_pref, LANES)
    vmem = 2 * tm * kd * 2 + 2 * kd * tn * 2 + 5 * tm * tn * 4
    if ring:
        return pl.pallas_call(
            functools.partial(_matmul_residual_ring_kernel, tm=tm, ni=m // tm),
            grid=(m // tm, n // tn),
            in_specs=[pl.BlockSpec(memory_space=pl.ANY),
                      pl.BlockSpec((None, kd, tn), lambda i, j: (li, 0, j)),
                      pl.BlockSpec((tm, tn), lambda i, j: (i, j))],
            out_specs=pl.BlockSpec((tm, tn), lambda i, j: (i, j)),
            out_shape=jax.ShapeDtypeStruct((m, n), F32),
            scratch_shapes=[pltpu.VMEM((2, tm, kd), BF16), pltpu.SemaphoreType.DMA((2,))],
            compiler_params=_params(("arbitrary", "arbitrary"), vmem),
            name="matmul_residual_ring",
        )(a, w, x)
    return pl.pallas_call(
        _matmul_residual_kernel,
        grid=(m // tm, n // tn),
        in_specs=[pl.BlockSpec((tm, kd), lambda i, j: (i, 0)),
                  pl.BlockSpec((None, kd, tn), lambda i, j: (li, 0, j)),
                  pl.BlockSpec((tm, tn), lambda i, j: (i, j))],
        out_specs=pl.BlockSpec((tm, tn), lambda i, j: (i, j)),
        out_shape=jax.ShapeDtypeStruct((m, n), F32),
        compiler_params=_params(("arbitrary", "arbitrary"), vmem),
        name="matmul_residual",
    )(a, w, x)


def _swa_kernel(sink_ref, q_ref, kc_ref, kp_ref, vc_ref, vp_ref, o_ref, *, group, hps):
    i, hb = pl.program_id(0), pl.program_id(1)
    blk = SWA_BLOCK
    nsub = q_ref.shape[0] // blk
    row = lax.broadcasted_iota(jnp.int32, (blk, 2 * blk), 0)
    col = lax.broadcasted_iota(jnp.int32, (blk, 2 * blk), 1)
    rel = col - blk
    band = (rel <= row) & (row - rel < blk)
    nt = (((1,), (1,)), ((), ()))
    for c in range(nsub):
        rows = slice(c * blk, (c + 1) * blk)
        if c == 0:
            mask = band & (rel >= jnp.where(i > 0, -blk, 0))
        else:
            prev = slice((c - 1) * blk, c * blk)
            mask = band
        for hh in range(hps):
            kcols = slice(hh * HEAD_DIM, (hh + 1) * HEAD_DIM)
            if c == 0:
                kprev, vprev = kp_ref[:, kcols], vp_ref[:, kcols]
            else:
                kprev, vprev = kc_ref[prev, kcols], vc_ref[prev, kcols]
            kw = jnp.concatenate([kprev, kc_ref[rows, kcols]], axis=0)
            vw = jnp.concatenate([vprev, vc_ref[rows, kcols]], axis=0)
            qcol = lambda g: slice((hh * group + g) * HEAD_DIM, (hh * group + g + 1) * HEAD_DIM)
            qs = jnp.concatenate([q_ref[rows, qcol(g)] for g in range(group)], axis=0)
            s = lax.dot_general(qs, kw, nt, preferred_element_type=F32)
            probs, inv = [], []
            for g in range(group):
                sg = jnp.where(mask, s[g * blk:(g + 1) * blk], -jnp.inf)
                sink = sink_ref[(hb * hps + hh) * group + g]
                mx = jnp.maximum(jnp.max(sg, axis=-1, keepdims=True), sink)
                p = jnp.exp(sg - mx)
                inv.append(1.0 / (jnp.sum(p, axis=-1, keepdims=True) + jnp.exp(sink - mx)))
                probs.append(p.astype(BF16))
            o = jnp.dot(jnp.concatenate(probs, axis=0), vw, preferred_element_type=F32)
            for g in range(group):
                o_ref[rows, qcol(g)] = (o[g * blk:(g + 1) * blk] * inv[g]).astype(o_ref.dtype)


def _swa_attention(qkv, sinks, *, hq, hkv, tq_pref=512):
    s = qkv.shape[0]
    group = hq // hkv
    hps = next(c for c in (4, 2, 1) if hkv % c == 0)
    tq = _tile(s, tq_pref, SWA_BLOCK)
    r = tq // SWA_BLOCK
    kw = hps * HEAD_DIM
    cur = lambda off: pl.BlockSpec((tq, kw), lambda i, h, sk: (i, off // hps + h))
    prev = lambda off: pl.BlockSpec((SWA_BLOCK, kw), lambda i, h, sk: (jnp.maximum(i * r - 1, 0), off // hps + h))
    vmem = (4 * tq * hps * group * HEAD_DIM * 2 + 8 * tq * kw * 2
            + 8 * hps * group * SWA_BLOCK * 2 * SWA_BLOCK * 4)
    return pl.pallas_call(
        functools.partial(_swa_kernel, group=group, hps=hps),
        grid_spec=pltpu.PrefetchScalarGridSpec(
            num_scalar_prefetch=1,
            grid=(s // tq, hkv // hps),
            in_specs=[pl.BlockSpec((tq, hps * group * HEAD_DIM), lambda i, h, sk: (i, h)),
                      cur(hq), prev(hq), cur(hq + hkv), prev(hq + hkv)],
            out_specs=pl.BlockSpec((tq, hps * group * HEAD_DIM), lambda i, h, sk: (i, h)),
        ),
        out_shape=jax.ShapeDtypeStruct((s, hq * HEAD_DIM), BF16),
        compiler_params=_params(("arbitrary", "arbitrary"), vmem),
        name="swa_attention",
    )(sinks, qkv, qkv, qkv, qkv, qkv)


def _diff_kernel(lam_ref, subln_ref, q_ref, k_ref, v_ref, o_ref, qs_ref, m_ref, l_ref, acc_ref, sa_ref, sb_ref,
                 *, group, tk, lambda_init):
    qi = pl.program_id(1)
    tq = q_ref.shape[0]
    rows = group * tq
    nt = (((1,), (1,)), ((), ()))

    for mp in range(2):
        for g in range(group):
            c = (g * 2 + mp) * HEAD_DIM
            qs_ref[mp, g * tq:(g + 1) * tq, :] = q_ref[:, c:c + HEAD_DIM]
    m_ref[...] = jnp.full(m_ref.shape, -jnp.inf, F32)
    l_ref[...] = jnp.zeros(l_ref.shape, F32)
    acc_ref[...] = jnp.zeros(acc_ref.shape, F32)

    def scores(kb, dst_ref):
        st = pl.multiple_of(kb * tk, tk)
        for mp in range(2):
            dst_ref[mp] = lax.dot_general(qs_ref[mp], k_ref[pl.ds(st, tk), mp * HEAD_DIM:(mp + 1) * HEAD_DIM], nt,
                                          preferred_element_type=F32)

    def update(kb, src_ref, nkeys, masked):
        st = pl.multiple_of(kb * tk, tk)
        vblk = v_ref[pl.ds(st, nkeys), :]
        if masked:
            qpos = qi * tq + (lax.broadcasted_iota(jnp.int32, (rows, nkeys), 0) & (tq - 1))
            kpos = st + lax.broadcasted_iota(jnp.int32, (rows, nkeys), 1)
            visible = kpos <= qpos
        for mp in range(2):
            s = src_ref[mp, :, :nkeys]
            if masked:
                s = jnp.where(visible, s, -jnp.inf)
            m_prev = m_ref[mp]
            m_next = jnp.maximum(m_prev, jnp.max(s, axis=-1, keepdims=True))
            alpha = jnp.exp2(m_prev - m_next)
            p = jnp.exp2(s - pltpu.repeat(m_next, nkeys // LANES, 1))
            lane_sum = p[:, :LANES]
            for t in range(1, nkeys // LANES):
                lane_sum = lane_sum + p[:, t * LANES:(t + 1) * LANES]
            l_ref[mp] = alpha * l_ref[mp] + lane_sum
            pv = jnp.dot(p.astype(BF16), vblk, preferred_element_type=F32)
            acc_ref[mp] = acc_ref[mp] * pltpu.repeat(alpha, acc_ref.shape[2] // LANES, 1) + pv
            m_ref[mp] = m_next

    n_full = (qi * tq) // tk
    scores(0, sa_ref)

    def body(t, carry):
        @pl.when(t % 2 == 0)
        def _():
            scores(t + 1, sb_ref)
            update(t, sa_ref, tk, False)

        @pl.when(t % 2 == 1)
        def _():
            scores(t + 1, sa_ref)
            update(t, sb_ref, tk, False)

        return carry

    lax.fori_loop(0, n_full, body, 0)

    sub = qi % (tk // tq)
    for k in range(tk // tq):
        for parity, buf in enumerate((sa_ref, sb_ref)):
            @pl.when((sub == k) & (n_full % 2 == parity))
            def _():
                update(n_full, buf, (k + 1) * tq, True)

    lam_vecs = lam_ref[...]
    lam = (jnp.exp(jnp.sum(lam_vecs[0:1] * lam_vecs[1:2], axis=-1, keepdims=True))
           - jnp.exp(jnp.sum(lam_vecs[2:3] * lam_vecs[3:4], axis=-1, keepdims=True)) + lambda_init)
    o = (acc_ref[0] / jnp.sum(l_ref[0], axis=-1, keepdims=True)
         - lam * (acc_ref[1] / jnp.sum(l_ref[1], axis=-1, keepdims=True)))
    y = _rms(o, subln_ref[...]) * (1.0 - lambda_init)
    vd = y.shape[1]
    for g in range(group):
        o_ref[:, g * vd:(g + 1) * vd] = y[g * tq:(g + 1) * tq].astype(o_ref.dtype)


def _diff_attention(qkv, lam_vecs, subln, li, *, heads, hkv, lambda_init, tq_pref=256, tk_pref=1024):
    s = qkv.shape[0]
    group = heads // hkv
    vd = 2 * HEAD_DIM
    tq = _tile(s, tq_pref, 8)
    tk = _tile(s, tk_pref, LANES)
    assert tk % tq == 0 and tq & (tq - 1) == 0, (tq, tk)
    rows = group * tq
    vmem = (2 * s * vd * 2 + 4 * tq * group * vd * 2 + 2 * rows * HEAD_DIM * 2
            + 4 * rows * LANES * 4 + 2 * rows * vd * 4 + 4 * rows * tk * 4 + 2 * rows * tk * 4)
    kern = functools.partial(_diff_kernel, group=group, tk=tk, lambda_init=lambda_init)
    resident = lambda off: pl.BlockSpec((s, vd), lambda h, i: (0, off + h), pipeline_mode=pl.Buffered(1))
    return pl.pallas_call(
        kern,
        grid=(hkv, s // tq),
        in_specs=[pl.BlockSpec((4, HEAD_DIM), lambda h, i: (0, 0)),
                  pl.BlockSpec((None, 1, vd), lambda h, i: (li, 0, 0)),
                  pl.BlockSpec((tq, group * vd), lambda h, i: (i, h)),
                  resident(heads), resident(heads + hkv)],
        out_specs=pl.BlockSpec((tq, group * vd), lambda h, i: (i, h)),
        out_shape=jax.ShapeDtypeStruct((s, heads * vd), BF16),
        scratch_shapes=[pltpu.VMEM((2, rows, HEAD_DIM), BF16),
                        pltpu.VMEM((2, rows, LANES), F32),
                        pltpu.VMEM((2, rows, LANES), F32),
                        pltpu.VMEM((2, rows, vd), F32),
                        pltpu.VMEM((2, rows, tk), F32),
                        pltpu.VMEM((2, rows, tk), F32)],
        compiler_params=_params(("arbitrary", "arbitrary"), vmem),
        name="diff_attention",
    )(lam_vecs, _row(subln), qkv, qkv, qkv)


def _pool_kernel(x_ref, xp_ref, g_ref, w_ref, scale_ref, o_ref, ext_ref):
    i = pl.program_id(0)
    tm = x_ref.shape[0]
    ch = w_ref.shape[1]
    x = x_ref[...]
    h = _rms(x, g_ref[...])
    hp = _rms(xp_ref[...], g_ref[...])
    ext_ref[0:POOL_HALO, :] = jnp.where(i > 0, hp, 0.0)
    ext_ref[POOL_HALO:, :] = h
    t = i * tm + lax.broadcasted_iota(jnp.int32, (tm, 1), 0)
    for gi, win in enumerate(POOL_WINDOWS):
        cols = slice(gi * ch, (gi + 1) * ch)
        e = ext_ref[:, cols]
        span = 1
        while span < win:
            e = e + pltpu.roll(e, span, 0)
            span *= 2
        count = jnp.minimum(t + 1, win).astype(F32)
        pooled = (e[POOL_HALO:] / count - h[:, cols]).astype(BF16)
        y = jnp.dot(pooled, w_ref[gi], preferred_element_type=F32)
        o_ref[:, cols] = x[:, cols] + y * scale_ref[:, cols]


def _pool_mixer(x, g, w, scale, li, *, tm_pref=256):
    s, d = x.shape
    groups, ch = w.shape[1], w.shape[2]
    assert groups == len(POOL_WINDOWS)
    tm = _tile(s, tm_pref, POOL_HALO)
    r = tm // POOL_HALO
    vec = pl.BlockSpec((None, 1, d), lambda i: (li, 0, 0))
    vmem = 4 * tm * d * 4 + 2 * groups * ch * ch * 2 + (tm + POOL_HALO) * d * 4 + 6 * tm * ch * 4
    return pl.pallas_call(
        _pool_kernel,
        grid=(s // tm,),
        in_specs=[pl.BlockSpec((tm, d), lambda i: (i, 0)),
                  pl.BlockSpec((POOL_HALO, d), lambda i: (jnp.maximum(i * r - 1, 0), 0)),
                  vec,
                  pl.BlockSpec((None, groups, ch, ch), lambda i: (li, 0, 0, 0)),
                  vec],
        out_specs=pl.BlockSpec((tm, d), lambda i: (i, 0)),
        out_shape=jax.ShapeDtypeStruct((s, d), F32),
        scratch_shapes=[pltpu.VMEM((tm + POOL_HALO, d), F32)],
        compiler_params=_params(("arbitrary",), vmem),
        name="pool_mixer",
    )(x, x, _row(g), w, _row(scale))


def _xattn_kernel(x_ref, g_ref, wq_ref, kv_ref, wo_ref, gn_ref, o_ref, hn_ref, *, heads):
    x = x_ref[...]
    h = _rms(x, g_ref[...]).astype(BF16)
    width = wq_ref.shape[1]
    hd = width // heads
    q = (jnp.dot(h, wq_ref[...], preferred_element_type=F32) * (hd ** -0.5)).astype(BF16)
    nt = (((1,), (1,)), ((), ()))
    outs = []
    for a in range(heads):
        cols = slice(a * hd, (a + 1) * hd)
        s = lax.dot_general(q[:, cols], kv_ref[:, cols], nt, preferred_element_type=F32)
        p = jnp.exp(s - jnp.max(s, axis=-1, keepdims=True))
        den = jnp.sum(p, axis=-1, keepdims=True)
        pv = jnp.dot(p.astype(BF16), kv_ref[:, width + a * hd:width + (a + 1) * hd], preferred_element_type=F32)
        outs.append((pv / den).astype(BF16))
    o = jnp.concatenate(outs, axis=-1)
    y = x + jnp.dot(o, wo_ref[...], preferred_element_type=F32)
    o_ref[...] = y
    hn_ref[...] = _rms(y, gn_ref[...]).astype(hn_ref.dtype)


def _xattn(x, g, wq, kv, wo, g_next, li, *, tm_pref=256):
    s, d = x.shape
    width = wq.shape[-1]
    mem_len = kv.shape[0]
    tm = _tile(s, tm_pref, 8)
    vmem = (4 * tm * d * 4 + 2 * tm * d * 2 + 2 * d * width * 2 + mem_len * 2 * width * 2 + tm * d * 6
            + 4 * tm * width * 4)
    layer = lambda shape: pl.BlockSpec((None,) + shape, lambda i: (li, 0, 0), pipeline_mode=pl.Buffered(1))
    rows = pl.BlockSpec((tm, d), lambda i: (i, 0))
    return pl.pallas_call(
        functools.partial(_xattn_kernel, heads=X_HEADS),
        grid=(s // tm,),
        in_specs=[rows, layer((1, d)), layer((d, width)),
                  pl.BlockSpec((mem_len, 2 * width), lambda i: (0, 0), pipeline_mode=pl.Buffered(1)),
                  layer((width, d)), layer((1, d))],
        out_specs=[rows, rows],
        out_shape=[jax.ShapeDtypeStruct((s, d), F32), jax.ShapeDtypeStruct((s, d), BF16)],
        compiler_params=_params(("arbitrary",), vmem),
        name="xattn",
    )(x, _row(g), wq, kv, wo, _row(g_next))


def _ffn_in_kernel(h_ref, wg_ref, wv_ref, cwg_ref, cwv_ref, cbg_ref, cbv_ref, o_ref, cg_ref, cv_ref, *, chunk):
    i, j = pl.program_id(0), pl.program_id(1)
    tm = h_ref.shape[0]

    @pl.when(i == 0)
    def _():
        cg_ref[j] = jnp.zeros(cg_ref.shape[1:], F32)
        cv_ref[j] = jnp.zeros(cv_ref.shape[1:], F32)

    def conv(w_ref, cw_ref, cb_ref, carry_ref, tail_rows=0):
        if tail_rows:
            cut = tm - tail_rows
            u = jnp.concatenate([jnp.dot(h_ref[:cut, :], w_ref[...], preferred_element_type=F32),
                                 jnp.dot(h_ref[cut:, :], w_ref[...], preferred_element_type=F32)], axis=0)
        else:
            u = jnp.dot(h_ref[...], w_ref[...], preferred_element_type=F32)
        first = carry_ref[j]
        carry_ref[j] = u[tm - CONV_HALO:, :]
        out = []
        for r0 in range(0, tm, chunk):
            prev = first if r0 == 0 else u[r0 - CONV_HALO:r0, :]
            ext = jnp.concatenate([prev, u[r0:r0 + chunk, :]], axis=0)
            c = cb_ref[...]
            for tap in range(CONV_WIDTH):
                lag = CONV_WIDTH - 1 - tap
                shifted = pltpu.roll(ext, lag, 0) if lag else ext
                c = c + cw_ref[tap:tap + 1, :] * shifted[CONV_HALO:, :]
            out.append(c)
        return out

    gate = conv(wg_ref, cwg_ref, cbg_ref, cg_ref)
    val = conv(wv_ref, cwv_ref, cbv_ref, cv_ref, tail_rows=chunk)
    for k, (gk, vk) in enumerate(zip(gate, val)):
        o_ref[k * chunk:(k + 1) * chunk, :] = (gk / (1.0 + jnp.exp(-gk)) * vk).astype(o_ref.dtype)


def _ffn_in(h, w_in, conv_w, conv_b, li, *, tm_pref=1024, tn_pref=768, chunk_pref=256):
    m, d = h.shape
    f = w_in.shape[-1] // 2
    tm, tn = _tile(m, tm_pref, 8), _tile(f, tn_pref, LANES)
    chunk = _tile(tm, chunk_pref, 8)
    nj = f // tn
    lo = lambda rows: pl.BlockSpec((None, rows, tn), lambda i, j: (li, 0, j))
    hi = lambda rows: pl.BlockSpec((None, rows, tn), lambda i, j: (li, 0, j + nj))
    vmem = 2 * tm * d * 2 + 4 * d * tn * 2 + 2 * tm * tn * 2 + 2 * nj * CONV_HALO * tn * 4 + 6 * tm * tn * 4
    return pl.pallas_call(
        functools.partial(_ffn_in_kernel, chunk=chunk),
        grid=(m // tm, nj),
        in_specs=[pl.BlockSpec((tm, d), lambda i, j: (i, 0)),
                  lo(d), hi(d), lo(CONV_WIDTH), hi(CONV_WIDTH), lo(1), hi(1)],
        out_specs=pl.BlockSpec((tm, tn), lambda i, j: (i, j)),
        out_shape=jax.ShapeDtypeStruct((m, f), BF16),
        scratch_shapes=[pltpu.VMEM((nj, CONV_HALO, tn), F32),
                        pltpu.VMEM((nj, CONV_HALO, tn), F32)],
        compiler_params=_params(("arbitrary", "arbitrary"), vmem),
        name="ffn_in",
    )(h, w_in, w_in, conv_w, conv_w, _row(conv_b), _row(conv_b))


def _final_norm_kernel(x_ref, g_ref, o_ref):
    o_ref[...] = _rms(x_ref[...], g_ref[...])


def _final_norm(x, g, *, tm_pref=512):
    s, d = x.shape
    tm = _tile(s, tm_pref, 8)
    return pl.pallas_call(
        _final_norm_kernel,
        grid=(s // tm,),
        in_specs=[pl.BlockSpec((tm, d), lambda i: (i, 0)), pl.BlockSpec((1, d), lambda i: (0, 0))],
        out_specs=pl.BlockSpec((tm, d), lambda i: (i, 0)),
        out_shape=jax.ShapeDtypeStruct((s, d), F32),
        compiler_params=_params(("arbitrary",), 6 * tm * d * 4),
        name="final_norm",
    )(x, g.reshape(1, d))


def kernel(x, mem, positions, swa_norm, swa_w_qkv, swa_sinks, swa_w_o, diff_norm, diff_w_qkv, diff_lambda_q1, diff_lambda_k1, diff_lambda_q2, diff_lambda_k2, diff_subln, diff_w_o, pool_norm, pool_w, pool_scale, xattn_norm, xattn_mem_norm, xattn_w_q, xattn_w_kv, xattn_w_o, ffn_norm, ffn_w_in, ffn_conv_w, ffn_conv_b, ffn_w_out, final_norm):
    batch, seq, d = x.shape
    assert batch == 1, "the kernels fold no batch dimension"
    depth = ffn_w_in.shape[0]
    xs = x.reshape(seq, d)
    mem2 = mem.reshape(mem.shape[1], d)
    tables = _rope_tables(positions)
    bf = lambda w: w.astype(BF16)
    swa_w_qkv, swa_w_o, diff_w_qkv, diff_w_o, pool_w = map(bf, (swa_w_qkv, swa_w_o, diff_w_qkv, diff_w_o, pool_w))
    xattn_w_q, xattn_w_kv, xattn_w_o, ffn_w_in, ffn_w_out = map(
        bf, (xattn_w_q, xattn_w_kv, xattn_w_o, ffn_w_in, ffn_w_out))

    swa_hq = d // HEAD_DIM
    swa_hkv = (swa_w_qkv.shape[-1] // HEAD_DIM - swa_hq) // 2
    diff_heads = d // (2 * HEAD_DIM)
    diff_hkv = (diff_w_qkv.shape[-1] // (2 * HEAD_DIM) - diff_heads) // 2

    for i in range(depth):
        kind, j = i % N_MIXERS, i // N_MIXERS
        if kind == 0:
            qkv = _qkv_rope(xs, swa_norm, swa_w_qkv, j, tables, n_q=swa_hq * HEAD_DIM, n_k=swa_hkv * HEAD_DIM,
                            q_variant=ROPE_Q_SWA)
            att = _swa_attention(qkv, swa_sinks[j], hq=swa_hq, hkv=swa_hkv)
            xs = _matmul_residual(att, swa_w_o, j, xs, tm_pref=1024, tn_pref=512)
        elif kind == 1:
            lambda_init = 0.8 - 0.6 * math.exp(-0.3 * i)
            qkv = _qkv_rope(xs, diff_norm, diff_w_qkv, j, tables, n_q=2 * diff_heads * HEAD_DIM,
                            n_k=2 * diff_hkv * HEAD_DIM, q_variant=ROPE_Q_DIFF)
            lam_vecs = jnp.stack([diff_lambda_q1[j], diff_lambda_k1[j], diff_lambda_q2[j], diff_lambda_k2[j]])
            att = _diff_attention(qkv, lam_vecs, diff_subln, j, heads=diff_heads, hkv=diff_hkv,
                                  lambda_init=lambda_init)
            xs = _matmul_residual(att, diff_w_o, j, xs, tm_pref=1024, tn_pref=512)
        else:
            xs = _pool_mixer(xs, pool_norm, pool_w, pool_scale, j)
        kv = _norm_matmul(mem2, xattn_mem_norm, xattn_w_kv, i)
        xs, hn = _xattn(xs, xattn_norm, xattn_w_q, kv, xattn_w_o, ffn_norm, i)
        act = _ffn_in(hn, ffn_w_in, ffn_conv_w, ffn_conv_b, i)
        xs = _matmul_residual(act, ffn_w_out, i, xs, tm_pref=512, tn_pref=512, ring=True)
    return _final_norm(xs, final_norm).reshape(batch, seq, d)
```
